```python
import math
import jax, jax.numpy as jnp
from jax import lax
import numpy as np


D_MODEL = 1024
BATCH = 8
SEQ = 4096
DEPTH = 4

HEAD_DIM = 64
S5_CH = D_MODEL // 4
S5_GROUP = 16
S5_NGROUPS = S5_CH // S5_GROUP
S5_STATE = 64
S5_DT_MIN = 1e-3
S5_DT_MAX = 1e-1
MOBA_HEADS = D_MODEL // 128
MOBA_W = MOBA_HEADS * HEAD_DIM
MOBA_BLOCK = 256
MOBA_TOPK = 3
MOBA_QCHUNK = 32
ROPE_THETA = 500000.0
ROPE_DIMS = HEAD_DIM // 4
RET_HEADS = D_MODEL // 256
RET_W = RET_HEADS * HEAD_DIM
RET_CHUNK = 128
RET_ANGLE_BASE = 10000.0
MIX_W = S5_CH + MOBA_W + RET_W
SPLIT_SIZES = (S5_CH, MOBA_W, MOBA_W, MOBA_W, RET_W, RET_W, RET_W, RET_W)
IN_W = sum(SPLIT_SIZES)
SPLIT_POINTS = tuple(sum(SPLIT_SIZES[:i + 1]) for i in range(len(SPLIT_SIZES) - 1))
PEER_KEYS = 128
PEER_EXPERTS = PEER_KEYS * PEER_KEYS
PEER_HEADS = 8
PEER_TOPK = 16
PEER_QDIM = 128
PEER_CHUNK = 128
EPS = 1e-6

kernel_name = 'hymba_s5_moba_retnet_peer'


def rms_unit(x):
    xf = x.astype(jnp.float32)
    return xf * lax.rsqrt(jnp.mean(xf * xf, axis=-1, keepdims=True) + EPS)


def rms_norm(x, g):
    return (rms_unit(x) * g.astype(jnp.float32)).astype(x.dtype)


def rotate(x, cos, sin):
    m = x.shape[-1] // 2
    x1, x2 = x[..., :m], x[..., m:]
    c = cos[None, :, None, :]
    s = sin[None, :, None, :]
    return jnp.concatenate([x1 * c - x2 * s, x2 * c + x1 * s], axis=-1)


def partial_rope(x, cos, sin):
    return jnp.concatenate([rotate(x[..., :ROPE_DIMS], cos, sin), x[..., ROPE_DIMS:]], axis=-1)


def _ssm_combine(e1, e2):
    a1r, a1i, b1r, b1i = e1
    a2r, a2i, b2r, b2i = e2
    return (a1r * a2r - a1i * a2i,
            a1r * a2i + a1i * a2r,
            a2r * b1r - a2i * b1i + b2r,
            a2r * b1i + a2i * b1r + b2i)


def s5_mixer(u, lam_re, lam_im, log_dt, b_re, b_im, c_re, c_im, d_skip, w_glu):
    bsz, s, _ = u.shape
    uf = u.astype(jnp.float32).reshape(bsz, s, S5_NGROUPS, S5_GROUP)
    lr = lam_re.astype(jnp.float32)
    li = lam_im.astype(jnp.float32)
    dt = jnp.exp(log_dt.astype(jnp.float32))[:, None]
    mag = jnp.exp(lr * dt)
    ar = mag * jnp.cos(li * dt)
    ai = mag * jnp.sin(li * dt)
    den = lr * lr + li * li
    fr = ((ar - 1.0) * lr + ai * li) / den
    fi = (ai * lr - (ar - 1.0) * li) / den
    br = b_re.astype(jnp.float32)
    bi = b_im.astype(jnp.float32)
    bbr = fr[..., None] * br - fi[..., None] * bi
    bbi = fr[..., None] * bi + fi[..., None] * br
    bu_r = jnp.einsum('bsgh,gph->bsgp', uf, bbr)
    bu_i = jnp.einsum('bsgh,gph->bsgp', uf, bbi)
    a_r = jnp.broadcast_to(ar, bu_r.shape)
    a_i = jnp.broadcast_to(ai, bu_i.shape)
    _, _, st_r, st_i = lax.associative_scan(_ssm_combine, (a_r, a_i, bu_r, bu_i), axis=1)
    y = (jnp.einsum('bsgp,ghp->bsgh', st_r, c_re.astype(jnp.float32))
         - jnp.einsum('bsgp,ghp->bsgh', st_i, c_im.astype(jnp.float32))
         + uf * d_skip.astype(jnp.float32).reshape(S5_NGROUPS, S5_GROUP))
    y = jax.nn.gelu(y.reshape(bsz, s, S5_CH))
    y = y * jax.nn.sigmoid(y @ w_glu.astype(jnp.float32))
    return y.astype(u.dtype)


def moba_mixer(q, k, v, gq, gk, cos, sin):
    bsz, s, _ = q.shape
    dtype = v.dtype
    hs = (bsz, s, MOBA_HEADS, HEAD_DIM)
    q = partial_rope(rms_unit(q.reshape(hs)) * gq.astype(jnp.float32), cos, sin).astype(dtype)
    k = partial_rope(rms_unit(k.reshape(hs)) * gk.astype(jnp.float32), cos, sin).astype(dtype)
    q = q.transpose(0, 2, 1, 3)
    k = k.transpose(0, 2, 1, 3)
    v = v.reshape(hs).transpose(0, 2, 1, 3)
    nb = -(-s // MOBA_BLOCK)
    pad = nb * MOBA_BLOCK - s
    padw = ((0, 0), (0, 0), (0, pad), (0, 0))
    kb = jnp.pad(k, padw).reshape(bsz, MOBA_HEADS, nb, MOBA_BLOCK, HEAD_DIM)
    vb = jnp.pad(v, padw).reshape(bsz, MOBA_HEADS, nb, MOBA_BLOCK, HEAD_DIM)
    kmean = jnp.mean(kb, axis=3)
    topk = min(MOBA_TOPK, nb)
    scale = HEAD_DIM ** -0.5
    gather = jax.vmap(jax.vmap(lambda blocks, ix: blocks[ix]))

    def chunk(c):
        start = c * MOBA_QCHUNK
        blk = start // MOBA_BLOCK
        qc = lax.dynamic_slice_in_dim(q, start, MOBA_QCHUNK, axis=2)
        gate = jnp.einsum('bhqd,bhnd->bhqn', qc, kmean).astype(jnp.float32)
        gate = jnp.where(jnp.arange(nb) < blk, gate, -jnp.inf)
        _, idx = lax.top_k(gate, topk)
        valid = jnp.arange(topk) < blk
        sk = gather(kb, idx)
        sv = gather(vb, idx)
        s_sel = jnp.einsum('bhqd,bhqjkd->bhqjk', qc, sk).astype(jnp.float32) * scale
        s_sel = jnp.where(valid[:, None], s_sel, -jnp.inf)
        s_sel = s_sel.reshape(bsz, MOBA_HEADS, MOBA_QCHUNK, topk * MOBA_BLOCK)
        ok = lax.dynamic_index_in_dim(kb, blk, axis=2, keepdims=False)
        ov = lax.dynamic_index_in_dim(vb, blk, axis=2, keepdims=False)
        s_own = jnp.einsum('bhqd,bhkd->bhqk', qc, ok).astype(jnp.float32) * scale
        qpos = start + jnp.arange(MOBA_QCHUNK)
        kpos = blk * MOBA_BLOCK + jnp.arange(MOBA_BLOCK)
        s_own = jnp.where(kpos[None, :] <= qpos[:, None], s_own, -jnp.inf)
        p = jax.nn.softmax(jnp.concatenate([s_sel, s_own], axis=-1), axis=-1)
        p_sel = p[..., :topk * MOBA_BLOCK].reshape(bsz, MOBA_HEADS, MOBA_QCHUNK, topk, MOBA_BLOCK).astype(dtype)
        p_own = p[..., topk * MOBA_BLOCK:].astype(dtype)
        return (jnp.einsum('bhqjk,bhqjkd->bhqd', p_sel, sv)
                + jnp.einsum('bhqk,bhkd->bhqd', p_own, ov))

    out = lax.map(chunk, jnp.arange(s // MOBA_QCHUNK))
    return out.transpose(1, 0, 3, 2, 4).reshape(hs)


def retention_mixer(q, k, v, g, cos, sin):
    bsz, s, _ = q.shape
    dtype = v.dtype
    hs = (bsz, s, RET_HEADS, HEAD_DIM)
    q = rotate(q.astype(jnp.float32).reshape(hs), cos, sin)
    k = rotate(k.astype(jnp.float32).reshape(hs), cos, sin) * (HEAD_DIM ** -0.5)
    v = v.astype(jnp.float32).reshape(hs)
    log_g = jnp.log(1.0 - 2.0 ** (-5.0 - jnp.arange(RET_HEADS, dtype=jnp.float32)))
    n = s // RET_CHUNK
    cs = (bsz, n, RET_CHUNK, RET_HEADS, HEAD_DIM)
    qc, kc, vc = q.reshape(cs), k.reshape(cs), v.reshape(cs)
    i = jnp.arange(RET_CHUNK, dtype=jnp.float32)
    diff = i[:, None] - i[None, :]
    dmask = jnp.where(diff >= 0, jnp.exp(jnp.maximum(diff, 0.0)[None] * log_g[:, None, None]), 0.0)
    inner = jnp.einsum('bnihd,bnjhd->bnhij', qc, kc) * dmask
    inner = jnp.einsum('bnhij,bnjhe->bnihe', inner, vc)
    zeta = jnp.exp((RET_CHUNK - 1.0 - i)[None, :] * log_g[:, None])
    kv = jnp.einsum('bnjhd,hj,bnjhe->nbhde', kc, zeta, vc)
    chunk_decay = jnp.exp(RET_CHUNK * log_g)[None, :, None, None]

    def step(state, kv_n):
        return chunk_decay * state + kv_n, state

    _, r_prev = lax.scan(step, jnp.zeros((bsz, RET_HEADS, HEAD_DIM, HEAD_DIM), jnp.float32), kv)
    xi = jnp.exp((i + 1.0)[None, :] * log_g[:, None])
    cross = jnp.einsum('bnihd,nbhde,hi->bnihe', qc, r_prev, xi)
    o = rms_unit((inner + cross).reshape(hs)).reshape(bsz, s, RET_W)
    return (jax.nn.silu(g.astype(jnp.float32)) * o).astype(dtype)


def peer_layer(h, w_q, sub_keys, u_emb, v_emb):
    bsz, s, d = h.shape
    t = bsz * s
    hf = h.reshape(t, d)
    q = (hf @ w_q).reshape(t, PEER_HEADS, 2, PEER_QDIM // 2)
    sc = jnp.einsum('thcd,hcnd->thcn', q, sub_keys).astype(jnp.float32)
    sv, si = lax.top_k(sc, PEER_TOPK)
    cand_s = (sv[:, :, 0, :, None] + sv[:, :, 1, None, :]).reshape(t, PEER_HEADS, PEER_TOPK * PEER_TOPK)
    cand_i = (si[:, :, 0, :, None] * PEER_KEYS + si[:, :, 1, None, :]).reshape(t, PEER_HEADS, PEER_TOPK * PEER_TOPK)
    fs, fp = lax.top_k(cand_s, PEER_TOPK)
    eidx = jnp.take_along_axis(cand_i, fp, axis=-1)
    gates = jax.nn.softmax(fs, axis=-1).astype(h.dtype)
    nch = t // PEER_CHUNK

    def chunk(args):
        xc, ic, gc = args
        ue = u_emb[ic]
        act = jax.nn.gelu(jnp.einsum('cd,chkd->chk', xc, ue))
        ve = v_emb[ic]
        return jnp.einsum('chk,chkd->cd', gc * act, ve)

    out = lax.map(chunk, (hf.reshape(nch, PEER_CHUNK, d),
                          eidx.reshape(nch, PEER_CHUNK, PEER_HEADS, PEER_TOPK),
                          gates.reshape(nch, PEER_CHUNK, PEER_HEADS, PEER_TOPK)))
    return out.reshape(bsz, s, d)


def setup_inputs(seed: int = 0) -> dict:
    key = jax.random.key(seed)
    ks = jax.random.split(key, 24)
    L, D, f32 = DEPTH, D_MODEL, jnp.float32
    nrm = lambda k, shp, sc: jax.random.normal(k, shp, f32) * sc
    gain = lambda k, n: 1.0 + 0.02 * jax.random.normal(k, (L, n), f32)
    return {
        'x': jax.random.normal(ks[0], (BATCH, SEQ, D), f32),
        'norm1_g': gain(ks[1], D),
        'w_in': nrm(ks[2], (L, D, IN_W), D ** -0.5),
        's5_lam_re': -0.5 + nrm(ks[3], (L, S5_NGROUPS, S5_STATE), 0.01),
        's5_lam_im': math.pi * jnp.arange(S5_STATE, dtype=f32) + nrm(ks[4], (L, S5_NGROUPS, S5_STATE), 0.01),
        's5_log_dt': jax.random.uniform(ks[5], (L, S5_NGROUPS), f32, math.log(S5_DT_MIN), math.log(S5_DT_MAX)),
        's5_b_re': nrm(ks[6], (L, S5_NGROUPS, S5_STATE, S5_GROUP), (2 * S5_GROUP) ** -0.5),
        's5_b_im': nrm(ks[7], (L, S5_NGROUPS, S5_STATE, S5_GROUP), (2 * S5_GROUP) ** -0.5),
        's5_c_re': nrm(ks[8], (L, S5_NGROUPS, S5_GROUP, S5_STATE), (2 * S5_STATE) ** -0.5),
        's5_c_im': nrm(ks[9], (L, S5_NGROUPS, S5_GROUP, S5_STATE), (2 * S5_STATE) ** -0.5),
        's5_d': nrm(ks[10], (L, S5_CH), 1.0),
        's5_w_glu': nrm(ks[11], (L, S5_CH, S5_CH), S5_CH ** -0.5),
        'moba_q_g': gain(ks[12], HEAD_DIM),
        'moba_k_g': gain(ks[13], HEAD_DIM),
        'mix_out_g': gain(ks[14], MIX_W),
        'w_out': nrm(ks[15], (L, MIX_W, D), 0.5 * MIX_W ** -0.5),
        'norm2_g': gain(ks[16], D),
        'peer_w_q': nrm(ks[17], (L, D, PEER_HEADS * PEER_QDIM), D ** -0.5),
        'peer_sub_keys': nrm(ks[18], (L, PEER_HEADS, 2, PEER_KEYS, PEER_QDIM // 2), (PEER_QDIM // 2) ** -0.5),
        'peer_u': nrm(ks[19], (L, PEER_EXPERTS, D), D ** -0.5),
        'peer_v': nrm(ks[20], (L, PEER_EXPERTS, D), D ** -0.5),
    }


def reference(x, norm1_g, w_in, s5_lam_re, s5_lam_im, s5_log_dt, s5_b_re, s5_b_im, s5_c_re, s5_c_im,
              s5_d, s5_w_glu, moba_q_g, moba_k_g, mix_out_g, w_out, norm2_g, peer_w_q, peer_sub_keys,
              peer_u, peer_v):
    s = x.shape[1]
    pos = jnp.arange(s, dtype=jnp.float32)
    inv = ROPE_THETA ** (-jnp.arange(0, ROPE_DIMS, 2, dtype=jnp.float32) / ROPE_DIMS)
    ang = pos[:, None] * inv[None, :]
    rope_cos, rope_sin = jnp.cos(ang), jnp.sin(ang)
    ret_inv = RET_ANGLE_BASE ** (-jnp.linspace(0.0, 1.0, HEAD_DIM // 2, dtype=jnp.float32))
    ret_ang = pos[:, None] * ret_inv[None, :]
    ret_cos, ret_sin = jnp.cos(ret_ang), jnp.sin(ret_ang)
    bsz = x.shape[0]
    for l in range(DEPTH):
        h = rms_norm(x, norm1_g[l])
        proj = h @ w_in[l]
        ua, qb, kb, vb, qc, kc, vc, gc = jnp.split(proj, SPLIT_POINTS, axis=-1)
        ya = s5_mixer(ua, s5_lam_re[l], s5_lam_im[l], s5_log_dt[l], s5_b_re[l], s5_b_im[l],
                      s5_c_re[l], s5_c_im[l], s5_d[l], s5_w_glu[l])
        ya = rms_unit(ya)
        yb = rms_unit(moba_mixer(qb, kb, vb, moba_q_g[l], moba_k_g[l], rope_cos, rope_sin))
        yb = yb.reshape(bsz, s, MOBA_W)
        yc = retention_mixer(qc, kc, vc, gc, ret_cos, ret_sin).astype(jnp.float32)
        y = jnp.concatenate([ya, yb, yc], axis=-1) * mix_out_g[l].astype(jnp.float32)
        x = x + (y.astype(x.dtype) @ w_out[l])
        h2 = rms_norm(x, norm2_g[l])
        x = x + peer_layer(h2, peer_w_q[l], peer_sub_keys[l], peer_u[l], peer_v[l])
    return x
```

```python
import functools
import math

import numpy as np
import jax
import jax.numpy as jnp
from jax import lax
from jax.experimental import pallas as pl
from jax.experimental.pallas import tpu as pltpu

D_MODEL = 1024
HEAD_DIM = 64
S5_CH = 256
S5_GROUP = 16
S5_NGROUPS = 16
S5_STATE = 64
S5_W = S5_NGROUPS * S5_STATE
MOBA_HEADS = 8
MOBA_W = 512
MOBA_BLOCK = 256
MOBA_TOPK = 3
ROPE_THETA = 500000.0
ROPE_DIMS = 16
RET_HEADS = 4
RET_W = 256
RET_ANGLE_BASE = 10000.0
MIX_W = S5_CH + MOBA_W + RET_W
IN_W = S5_CH + 3 * MOBA_W + 4 * RET_W
PEER_KEYS = 128
PEER_HEADS = 8
PEER_TOPK = 16
PEER_QDIM = 128
PEER_SEL = PEER_HEADS * PEER_TOPK
EPS = 1e-6

LANES = 128
TABLE_ROWS_PER_EXPERT = D_MODEL // (2 * LANES)
TILE_STRIDE = 136
VMEM_LIMIT = 56 * 1024 * 1024

S5_CHUNK = 128
RET_CHUNK = 256
PROJ_ROWS = 256
PEER_ROUTE_TOKENS = 256
PEER_GATHER_TOKENS = 128

BF16 = jnp.bfloat16
F32 = jnp.float32
NEG_INF = float("-inf")


def _params(sem):
    return pltpu.CompilerParams(dimension_semantics=sem, vmem_limit_bytes=VMEM_LIMIT)


def _gelu_tanh(y):
    return 0.5 * y * (1.0 + jnp.tanh(0.7978845608028654 * (y + 0.044715 * (y * y * y))))


def _rms_scale(x, axis):
    return lax.rsqrt(jnp.mean(x * x, axis=axis, keepdims=True) + EPS)


def _split(a):
    hi = a.astype(BF16)
    return hi, (a - hi.astype(F32)).astype(BF16)


def _dot3(a, b):
    a_hi, a_lo = _split(a)
    b_hi, b_lo = _split(b)
    out = jnp.dot(a_hi, b_hi, preferred_element_type=F32)
    out = out + jnp.dot(a_hi, b_lo, preferred_element_type=F32)
    return out + jnp.dot(a_lo, b_hi, preferred_element_type=F32)


def _in_proj_kernel(x_ref, g_ref, w_ref, wlo_ref, ua_ref, mb_ref, rt_ref):
    x = x_ref[...]
    hf = x * _rms_scale(x, -1) * g_ref[...]
    h = hf.astype(BF16)
    ua_ref[...] = jnp.dot(h, w_ref[:, 0:S5_CH], preferred_element_type=F32)
    h_lo = (hf - h.astype(F32)).astype(BF16)
    q0, q1 = S5_CH, S5_CH + 2 * MOBA_W
    qk = jnp.dot(h, w_ref[:, q0:q1], preferred_element_type=F32)
    qk = qk + jnp.dot(h, wlo_ref[...], preferred_element_type=F32)
    qk = qk + jnp.dot(h_lo, w_ref[:, q0:q1], preferred_element_type=F32)
    mb_ref[:, 0:2 * MOBA_W] = qk
    mb_ref[:, 2 * MOBA_W:3 * MOBA_W] = jnp.dot(h, w_ref[:, q1:q1 + MOBA_W], preferred_element_type=F32)
    rt_ref[...] = jnp.dot(h, w_ref[:, S5_CH + 3 * MOBA_W:IN_W], preferred_element_type=F32)


def _in_proj(x2, g, w, wlo):
    t = x2.shape[0]
    tm = PROJ_ROWS
    return pl.pallas_call(
        _in_proj_kernel,
        grid=(t // tm,),
        in_specs=[pl.BlockSpec((tm, D_MODEL), lambda i: (i, 0)),
                  pl.BlockSpec((1, D_MODEL), lambda i: (0, 0)),
                  pl.BlockSpec((D_MODEL, IN_W), lambda i: (0, 0)),
                  pl.BlockSpec((D_MODEL, 2 * MOBA_W), lambda i: (0, 0))],
        out_specs=[pl.BlockSpec((tm, S5_CH), lambda i: (i, 0)),
                   pl.BlockSpec((tm, 3 * MOBA_W), lambda i: (i, 0)),
                   pl.BlockSpec((tm, 4 * RET_W), lambda i: (i, 0))],
        out_shape=[jax.ShapeDtypeStruct((t, S5_CH), F32),
                   jax.ShapeDtypeStruct((t, 3 * MOBA_W), F32),
                   jax.ShapeDtypeStruct((t, 4 * RET_W), F32)],
        compiler_params=_params(("parallel",)),
        name="in_proj",
    )(x2, g, w, wlo)


def _s5_kernel(u_ref, bcat_ref, ccat_ref, tab_ref, d_ref, wglu_ref, o_ref, st_ref):
    L = S5_CHUNK

    @pl.when(pl.program_id(1) == 0)
    def _():
        st_ref[...] = jnp.zeros_like(st_ref)

    u = u_ref[...]
    bu = jnp.dot(u.astype(BF16), bcat_ref[...], preferred_element_type=F32)
    bur, bui = bu[:, :S5_W], bu[:, S5_W:]
    air, aii = tab_ref[0], tab_ref[1]
    ktr = bur * air - bui * aii
    kti = bur * aii + bui * air
    row = lax.broadcasted_iota(jnp.int32, (L, L), 0)
    col = lax.broadcasted_iota(jnp.int32, (L, L), 1)
    tri = jnp.where(row >= col, 1.0, 0.0).astype(BF16)
    cr = jnp.dot(tri, ktr.astype(BF16), preferred_element_type=F32)
    ci = jnp.dot(tri, kti.astype(BF16), preferred_element_type=F32)
    apr, api = tab_ref[2], tab_ref[3]
    acr, aci = tab_ref[4], tab_ref[5]
    pr = st_ref[0:1, :]
    pi = st_ref[1:2, :]
    sr = apr * cr - api * ci + acr * pr - aci * pi
    si = apr * ci + api * cr + acr * pi + aci * pr
    st_ref[0:1, :] = sr[L - 1:L, :]
    st_ref[1:2, :] = si[L - 1:L, :]
    y = (jnp.dot(sr.astype(BF16), ccat_ref[0:S5_W, :], preferred_element_type=F32)
         + jnp.dot(si.astype(BF16), ccat_ref[S5_W:2 * S5_W, :], preferred_element_type=F32))
    y = _gelu_tanh(y + u * d_ref[...])
    z = jnp.dot(y.astype(BF16), wglu_ref[...], preferred_element_type=F32)
    y = y * jax.nn.sigmoid(z)
    o_ref[...] = y * _rms_scale(y, -1)


def _s5_operands(lam_re, lam_im, log_dt, b_re, b_im, c_re, c_im):
    dt = jnp.exp(log_dt)[:, None]
    mag = jnp.exp(lam_re * dt)
    ar = mag * jnp.cos(lam_im * dt)
    ai = mag * jnp.sin(lam_im * dt)
    den = lam_re * lam_re + lam_im * lam_im
    fr = ((ar - 1.0) * lam_re + ai * lam_im) / den
    fi = (ai * lam_re - (ar - 1.0) * lam_im) / den
    bbr = fr[..., None] * b_re - fi[..., None] * b_im
    bbi = fr[..., None] * b_im + fi[..., None] * b_re
    eye = jnp.eye(S5_NGROUPS, dtype=F32)
    blk = lambda m: jnp.einsum('gph,gk->ghkp', m, eye).reshape(S5_CH, S5_W)
    bcat = jnp.concatenate([blk(bbr), blk(bbi)], axis=1).astype(BF16)
    blc = lambda m: jnp.einsum('ghp,gk->gpkh', m, eye).reshape(S5_W, S5_CH)
    ccat = jnp.concatenate([blc(c_re), -blc(c_im)], axis=0).astype(BF16)

    def powers(n):
        lr = (lam_re * dt).reshape(1, S5_W)
        li = (lam_im * dt).reshape(1, S5_W)
        m = jnp.exp(lr * n[:, None])
        return m * jnp.cos(li * n[:, None]), m * jnp.sin(li * n[:, None])

    j = jnp.arange(S5_CHUNK, dtype=F32)
    tabs = jnp.stack([*powers(-j), *powers(j), *powers(j + 1.0)], axis=0)
    return bcat, ccat, tabs


def _s5_mixer(ua, bsz, s, bcat, ccat, tabs, d_skip, w_glu):
    L = S5_CHUNK
    nc = s // L
    return pl.pallas_call(
        _s5_kernel,
        grid=(bsz, nc),
        in_specs=[pl.BlockSpec((L, S5_CH), lambda b, j: (b * nc + j, 0)),
                  pl.BlockSpec((S5_CH, 2 * S5_W), lambda b, j: (0, 0)),
                  pl.BlockSpec((2 * S5_W, S5_CH), lambda b, j: (0, 0)),
                  pl.BlockSpec((6, L, S5_W), lambda b, j: (0, 0, 0)),
                  pl.BlockSpec((1, S5_CH), lambda b, j: (0, 0)),
                  pl.BlockSpec((S5_CH, S5_CH), lambda b, j: (0, 0))],
        out_specs=pl.BlockSpec((L, S5_CH), lambda b, j: (b * nc + j, 0)),
        out_shape=jax.ShapeDtypeStruct((bsz * s, S5_CH), F32),
        scratch_shapes=[pltpu.VMEM((2, S5_W), F32)],
        compiler_params=_params(("parallel", "arbitrary")),
        name="s5_mixer",
    )(ua, bcat, ccat, tabs, d_skip, w_glu)


def _moba_prep_kernel(qkv_ref, gq_ref, gk_ref, cos_ref, sin_ref, qT_ref, k_ref, vT_ref, gate_ref, km_ref):
    j = pl.program_id(1)

    @pl.when(j == 0)
    def _():
        km_ref[...] = jnp.zeros_like(km_ref)

    x = qkv_ref[...]
    c = cos_ref[...]
    s = sin_ref[...]
    half = ROPE_DIMS // 2

    def prep(xT, g):
        y = xT * _rms_scale(xT, 0) * g
        x1 = y[0:half]
        x2 = y[half:ROPE_DIMS]
        return jnp.concatenate([x1 * c - x2 * s, x2 * c + x1 * s, y[ROPE_DIMS:]], axis=0)

    qT_all = x[:, 0:MOBA_W].T
    kT_all = x[:, MOBA_W:2 * MOBA_W].T
    vT_all = x[:, 2 * MOBA_W:3 * MOBA_W].T
    gq = gq_ref[...]
    gk = gk_ref[...]
    for h in range(MOBA_HEADS):
        sl = slice(h * HEAD_DIM, (h + 1) * HEAD_DIM)
        q = prep(qT_all[sl], gq) * (HEAD_DIM ** -0.5)
        qT_ref[h] = q.astype(BF16)
        gate_ref[h] = _dot3(km_ref[h], q)
        k = prep(kT_all[sl], gk).T
        k_ref[h] = k.astype(BF16)
        km_ref[h, pl.ds(j, 1), :] = jnp.mean(k, axis=0, keepdims=True)
        vT_ref[h] = vT_all[sl].astype(BF16)


def _moba_prep(qkv, bsz, s, gq, gk, cosT, sinT):
    nb = s // MOBA_BLOCK
    H, dh, blk = MOBA_HEADS, HEAD_DIM, MOBA_BLOCK
    return pl.pallas_call(
        _moba_prep_kernel,
        grid=(bsz, nb),
        in_specs=[pl.BlockSpec((blk, 3 * MOBA_W), lambda b, j: (b * nb + j, 0)),
                  pl.BlockSpec((dh, 1), lambda b, j: (0, 0)),
                  pl.BlockSpec((dh, 1), lambda b, j: (0, 0)),
                  pl.BlockSpec((ROPE_DIMS // 2, blk), lambda b, j: (0, j)),
                  pl.BlockSpec((ROPE_DIMS // 2, blk), lambda b, j: (0, j))],
        out_specs=[pl.BlockSpec((None, H, dh, blk), lambda b, j: (b, 0, 0, j)),
                   pl.BlockSpec((None, H, None, blk, dh), lambda b, j: (b, 0, j, 0, 0)),
                   pl.BlockSpec((None, H, None, dh, blk), lambda b, j: (b, 0, j, 0, 0)),
                   pl.BlockSpec((None, H, nb, blk), lambda b, j: (b, 0, 0, j))],
        out_shape=[jax.ShapeDtypeStruct((bsz, H, dh, s), BF16),
                   jax.ShapeDtypeStruct((bsz, H, nb, blk, dh), BF16),
                   jax.ShapeDtypeStruct((bsz, H, nb, dh, blk), BF16),
                   jax.ShapeDtypeStruct((bsz, H, nb, s), F32)],
        scratch_shapes=[pltpu.VMEM((H, nb, dh), F32)],
        compiler_params=_params(("parallel", "arbitrary")),
        name="moba_prep",
    )(qkv, gq, gk, cosT, sinT)


def _moba_attn_kernel(qT_ref, k_ref, vT_ref, gate_ref, o_ref, msk_ref):
    i = pl.program_id(2)
    nb = gate_ref.shape[0]
    blk = MOBA_BLOCK
    qT = qT_ref[...]
    gate = gate_ref[...]
    row = lax.broadcasted_iota(jnp.int32, (nb, blk), 0)
    gm = jnp.where(row < i, gate, NEG_INF)
    cnt = jnp.zeros((nb, blk), jnp.int32)
    for m in range(nb):
        gmm = gm[m:m + 1, :]
        beats = jnp.where(gmm > gm, 1, jnp.where(gmm == gm, jnp.where(row > m, 1, 0), 0))
        cnt = cnt + beats
    sel = jnp.where(row < i, jnp.where(cnt < MOBA_TOPK, 0.0, NEG_INF), NEG_INF)
    msk_ref[...] = sel

    kpos = lax.broadcasted_iota(jnp.int32, (blk, blk), 0)
    qpos = lax.broadcasted_iota(jnp.int32, (blk, blk), 1)
    sT = jnp.dot(k_ref[i], qT, preferred_element_type=F32)
    sT = jnp.where(kpos <= qpos, sT, NEG_INF)
    m0 = jnp.max(sT, axis=0, keepdims=True)
    p = jnp.exp(sT - m0)
    l0 = jnp.sum(p, axis=0, keepdims=True)
    acc0 = jnp.dot(vT_ref[i], p.astype(BF16), preferred_element_type=F32)

    def body(n, carry):
        m, l, acc = carry
        sT = jnp.dot(k_ref[n], qT, preferred_element_type=F32) + msk_ref[pl.ds(n, 1), :]
        m_new = jnp.maximum(m, jnp.max(sT, axis=0, keepdims=True))
        alpha = jnp.exp(m - m_new)
        p = jnp.exp(sT - m_new)
        l = alpha * l + jnp.sum(p, axis=0, keepdims=True)
        acc = alpha * acc + jnp.dot(vT_ref[n], p.astype(BF16), preferred_element_type=F32)
        return m_new, l, acc

    _, l, acc = lax.fori_loop(0, i, body, (m0, l0, acc0))
    o_ref[...] = acc / l


def _moba_attn(qT, k, vT, gate):
    bsz, H, dh, s = qT.shape
    nb = s // MOBA_BLOCK
    blk = MOBA_BLOCK
    return pl.pallas_call(
        _moba_attn_kernel,
        grid=(bsz, H, nb),
        in_specs=[pl.BlockSpec((None, None, dh, blk), lambda b, h, i: (b, h, 0, i)),
                  pl.BlockSpec((None, None, nb, blk, dh), lambda b, h, i: (b, h, 0, 0, 0)),
                  pl.BlockSpec((None, None, nb, dh, blk), lambda b, h, i: (b, h, 0, 0, 0)),
                  pl.BlockSpec((None, None, nb, blk), lambda b, h, i: (b, h, 0, i))],
        out_specs=pl.BlockSpec((None, None, dh, blk), lambda b, h, i: (b, h, 0, i)),
        out_shape=jax.ShapeDtypeStruct((bsz, H, dh, s), F32),
        scratch_shapes=[pltpu.VMEM((nb, blk), F32)],
        compiler_params=_params(("parallel", "parallel", "arbitrary")),
        name="moba_attn",
    )(qT, k, vT, gate)


def _ret_log_decay():
    return np.log(1.0 - 2.0 ** (-5.0 - np.arange(RET_HEADS, dtype=np.float64)))


def _ret_kernel(x_ref, cos_ref, sin_ref, xi_ref, zeta_ref, dm_ref, o_ref, st_ref):
    C = RET_CHUNK

    @pl.when(pl.program_id(1) == 0)
    def _():
        st_ref[...] = jnp.zeros_like(st_ref)

    x = x_ref[...]
    q = x[:, 0:RET_W]
    k = x[:, RET_W:2 * RET_W]
    v = x[:, 2 * RET_W:3 * RET_W]
    g = x[:, 3 * RET_W:4 * RET_W]
    cos = cos_ref[...]
    sin = sin_ref[...]
    lane = lax.broadcasted_iota(jnp.int32, (C, RET_W), 1)
    first = (lane & (HEAD_DIM // 2)) == 0

    def rot(t):
        swapped = jnp.where(first, pltpu.roll(t, RET_W - HEAD_DIM // 2, 1), pltpu.roll(t, HEAD_DIM // 2, 1))
        return t * cos + swapped * sin

    qr = rot(q)
    kr = rot(k) * (HEAD_DIM ** -0.5)
    qb = qr.astype(BF16)
    kb = kr.astype(BF16)
    vb = v.astype(BF16)
    qx = (qr * xi_ref[...]).astype(BF16)
    kz = kr * zeta_ref[...]
    decay_c = np.exp(C * _ret_log_decay())
    for h in range(RET_HEADS):
        sl = slice(h * HEAD_DIM, (h + 1) * HEAD_DIM)
        sc = lax.dot_general(qb[:, sl], kb[:, sl], (((1,), (1,)), ((), ())),
                             preferred_element_type=F32) * dm_ref[h]
        inner = jnp.dot(sc.astype(BF16), vb[:, sl], preferred_element_type=F32)
        r_prev = st_ref[h]
        cross = jnp.dot(qx[:, sl], r_prev.astype(BF16), preferred_element_type=F32)
        kv = jnp.dot(kz[:, sl].T.astype(BF16), vb[:, sl], preferred_element_type=F32)
        st_ref[h] = float(decay_c[h]) * r_prev + kv
        o = inner + cross
        o = o * _rms_scale(o, -1)
        gh = g[:, sl]
        o_ref[:, sl] = gh * jax.nn.sigmoid(gh) * o


def _ret_tables(s):
    C = RET_CHUNK
    half = HEAD_DIM // 2
    pos = jnp.arange(s, dtype=F32)
    inv = RET_ANGLE_BASE ** (-jnp.linspace(0.0, 1.0, half, dtype=F32))
    ang = pos[:, None] * inv[None, :]
    cos, sin = jnp.cos(ang), jnp.sin(ang)
    cos_t = jnp.tile(cos, (1, 2 * RET_HEADS))
    sin_t = jnp.tile(jnp.concatenate([-sin, sin], axis=1), (1, RET_HEADS))
    lg = _ret_log_decay()
    i = np.arange(C, dtype=np.float64)
    xi = np.repeat(np.exp((i + 1.0)[:, None] * lg[None, :]), HEAD_DIM, axis=1)
    zeta = np.repeat(np.exp((C - 1.0 - i)[:, None] * lg[None, :]), HEAD_DIM, axis=1)
    diff = i[:, None] - i[None, :]
    dm = np.where(diff >= 0, np.exp(np.maximum(diff, 0.0)[None] * lg[:, None, None]), 0.0)
    return cos_t, sin_t, jnp.asarray(xi, F32), jnp.asarray(zeta, F32), jnp.asarray(dm, F32)


def _ret_mixer(rt, bsz, s, tables):
    C = RET_CHUNK
    nc = s // C
    cos_t, sin_t, xi, zeta, dm = tables
    return pl.pallas_call(
        _ret_kernel,
        grid=(bsz, nc),
        in_specs=[pl.BlockSpec((C, 4 * RET_W), lambda b, j: (b * nc + j, 0)),
                  pl.BlockSpec((C, RET_W), lambda b, j: (j, 0)),
                  pl.BlockSpec((C, RET_W), lambda b, j: (j, 0)),
                  pl.BlockSpec((C, RET_W), lambda b, j: (0, 0)),
                  pl.BlockSpec((C, RET_W), lambda b, j: (0, 0)),
                  pl.BlockSpec((RET_HEADS, C, C), lambda b, j: (0, 0, 0))],
        out_specs=pl.BlockSpec((C, RET_W), lambda b, j: (b * nc + j, 0)),
        out_shape=jax.ShapeDtypeStruct((bsz * s, RET_W), F32),
        scratch_shapes=[pltpu.VMEM((RET_HEADS, HEAD_DIM, HEAD_DIM), F32)],
        compiler_params=_params(("parallel", "arbitrary")),
        name="ret_mixer",
    )(rt, cos_t, sin_t, xi, zeta, dm)


def _out_proj_kernel(x_ref, ya_ref, ot_ref, yc_ref, g_ref, w_ref, o_ref):
    ts = x_ref.shape[0]
    ot = ot_ref[...]
    yb = (ot * _rms_scale(ot, 1)).reshape(MOBA_W, ts).T
    g = g_ref[...]
    a0, a1 = S5_CH, S5_CH + MOBA_W
    acc = x_ref[...]
    acc = acc + jnp.dot((ya_ref[...] * g[:, 0:a0]).astype(BF16), w_ref[0:a0, :], preferred_element_type=F32)
    acc = acc + jnp.dot((yb * g[:, a0:a1]).astype(BF16), w_ref[a0:a1, :], preferred_element_type=F32)
    acc = acc + jnp.dot((yc_ref[...] * g[:, a1:MIX_W]).astype(BF16), w_ref[a1:MIX_W, :],
                        preferred_element_type=F32)
    o_ref[...] = acc


def _out_proj(x2, ya, ot, yc, g, w, bsz, s):
    ts = PROJ_ROWS
    nt = s // ts
    return pl.pallas_call(
        _out_proj_kernel,
        grid=(bsz, nt),
        in_specs=[pl.BlockSpec((ts, D_MODEL), lambda b, j: (b * nt + j, 0)),
                  pl.BlockSpec((ts, S5_CH), lambda b, j: (b * nt + j, 0)),
                  pl.BlockSpec((None, MOBA_HEADS, HEAD_DIM, ts), lambda b, j: (b, 0, 0, j)),
                  pl.BlockSpec((ts, RET_W), lambda b, j: (b * nt + j, 0)),
                  pl.BlockSpec((1, MIX_W), lambda b, j: (0, 0)),
                  pl.BlockSpec((MIX_W, D_MODEL), lambda b, j: (0, 0))],
        out_specs=pl.BlockSpec((ts, D_MODEL), lambda b, j: (b * nt + j, 0)),
        out_shape=jax.ShapeDtypeStruct((bsz * s, D_MODEL), F32),
        compiler_params=_params(("parallel", "parallel")),
        name="out_proj",
    )(x2, ya, ot, yc, g, w)


def _top16(s, payload):
    n = s.shape[0]
    io = lax.broadcasted_iota(jnp.int32, s.shape, 0)
    vals, pays = [], []
    for _ in range(PEER_TOPK):
        m = jnp.max(s, axis=0, keepdims=True)
        pos = jnp.min(jnp.where(s == m, io, n), axis=0, keepdims=True)
        hit = io == pos
        vals.append(m)
        pays.append(jnp.sum(jnp.where(hit, payload, 0), axis=0, keepdims=True))
        s = jnp.where(hit, NEG_INF, s)
    return jnp.concatenate(vals, axis=0), jnp.concatenate(pays, axis=0)


def _peer_route_kernel(x_ref, g_ref, wqT_ref, sk_ref, h2_ref, idx_ref, gate_ref, q_scr):
    tb = x_ref.shape[0]
    x = x_ref[...]
    h2 = x * _rms_scale(x, -1) * g_ref[...]
    h2_ref[...] = h2
    q_scr[...] = lax.dot_general(wqT_ref[...], h2.astype(BF16), (((1,), (1,)), ((), ())),
                                 preferred_element_type=F32).astype(BF16)
    half = PEER_QDIM // 2
    io_keys = lax.broadcasted_iota(jnp.int32, (PEER_KEYS, tb), 0)

    def head(h, carry):
        sub = []
        for c in range(2):
            grp = 2 * h + c
            qg = q_scr[pl.ds(pl.multiple_of(grp * half, half), half), :]
            sc = jnp.dot(sk_ref[grp], qg, preferred_element_type=F32)
            sub.append(_top16(sc, io_keys))
        (v0, i0), (v1, i1) = sub
        cand = jnp.concatenate([v0[a:a + 1, :] + v1 for a in range(PEER_TOPK)], axis=0)
        cid = jnp.concatenate([i0[a:a + 1, :] * PEER_KEYS + i1 for a in range(PEER_TOPK)], axis=0)
        fs, eidx = _top16(cand, cid)
        e = jnp.exp(fs - fs[0:1, :])
        gates = e / jnp.sum(e, axis=0, keepdims=True)
        r0 = pl.multiple_of(h * PEER_TOPK, PEER_TOPK)
        idx_ref[pl.ds(r0, PEER_TOPK), :] = eidx
        gate_ref[pl.ds(r0, PEER_TOPK), :] = gates
        return carry

    lax.fori_loop(0, PEER_HEADS, head, 0)


def _peer_route(x2, g, wqT, sk):
    t = x2.shape[0]
    tb = PEER_ROUTE_TOKENS
    ngrp = 2 * PEER_HEADS
    return pl.pallas_call(
        _peer_route_kernel,
        grid=(t // tb,),
        in_specs=[pl.BlockSpec((tb, D_MODEL), lambda i: (i, 0)),
                  pl.BlockSpec((1, D_MODEL), lambda i: (0, 0)),
                  pl.BlockSpec((PEER_HEADS * PEER_QDIM, D_MODEL), lambda i: (0, 0)),
                  pl.BlockSpec((ngrp, PEER_KEYS, PEER_QDIM // 2), lambda i: (0, 0, 0))],
        out_specs=[pl.BlockSpec((tb, D_MODEL), lambda i: (i, 0)),
                   pl.BlockSpec((PEER_SEL, tb), lambda i: (0, i)),
                   pl.BlockSpec((PEER_SEL, tb), lambda i: (0, i))],
        out_shape=[jax.ShapeDtypeStruct((t, D_MODEL), F32),
                   jax.ShapeDtypeStruct((PEER_SEL, t), jnp.int32),
                   jax.ShapeDtypeStruct((PEER_SEL, t), F32)],
        scratch_shapes=[pltpu.VMEM((PEER_HEADS * PEER_QDIM, tb), BF16)],
        compiler_params=_params(("parallel",)),
        name="peer_route",
    )(x2, g, wqT, sk)


def _pack_table(tab):
    bits = lax.bitcast_convert_type(tab.astype(BF16), jnp.uint16).astype(jnp.uint32)
    half = D_MODEL // 2
    words = bits[:, :half] | (bits[:, half:] << 16)
    return lax.bitcast_convert_type(words, jnp.int32).reshape(tab.shape[0] * TABLE_ROWS_PER_EXPERT, LANES)


def _unpack(words):
    lo = lax.bitcast_convert_type(words << 16, F32)
    hi = lax.bitcast_convert_type(words & jnp.int32(-65536), F32)
    return lo, hi


def _table_spec(tab):
    return pl.BlockSpec(tab.shape, lambda i: (0, 0), pipeline_mode=pl.Buffered(1))


def _peer_act_kernel(idx_ref, x_ref, gate_ref, tab_vmem, c_ref, tile_ref):
    tb = x_ref.shape[0]
    R = TABLE_ROWS_PER_EXPERT
    lane = lax.broadcasted_iota(jnp.int32, (PEER_SEL, tb), 1)

    def token(t, out_t):
        for k in range(PEER_SEL):
            e = idx_ref[k, t]
            slab = tab_vmem[pl.ds(pl.multiple_of(e * R, R), R), :]
            tile_ref[pl.ds(k, R, stride=TILE_STRIDE), :] = slab
        xr = x_ref[pl.ds(t, 1), :]
        acc = jnp.zeros((PEER_SEL, LANES), F32)
        for r in range(R):
            lo, hi = _unpack(tile_ref[r * TILE_STRIDE:r * TILE_STRIDE + PEER_SEL, :])
            acc = acc + lo * xr[:, r * LANES:(r + 1) * LANES]
            acc = acc + hi * xr[:, (R + r) * LANES:(R + r + 1) * LANES]
        col = jnp.sum(acc, axis=1, keepdims=True)
        return jnp.where(lane == t, col, out_t)

    act = lax.fori_loop(0, tb, token, jnp.zeros((PEER_SEL, tb), F32))
    c_ref[...] = gate_ref[...] * _gelu_tanh(act)


def _peer_act(idx_t, h2, gate_t, tab):
    t = h2.shape[0]
    tb = PEER_GATHER_TOKENS
    return pl.pallas_call(
        _peer_act_kernel,
        grid=(t // tb,),
        in_specs=[pl.BlockSpec((PEER_SEL, tb), lambda i: (0, i), memory_space=pltpu.SMEM),
                  pl.BlockSpec((tb, D_MODEL), lambda i: (i, 0)),
                  pl.BlockSpec((PEER_SEL, tb), lambda i: (0, i)),
                  _table_spec(tab)],
        out_specs=pl.BlockSpec((PEER_SEL, tb), lambda i: (0, i)),
        out_shape=jax.ShapeDtypeStruct((PEER_SEL, t), F32),
        scratch_shapes=[pltpu.VMEM((TABLE_ROWS_PER_EXPERT * TILE_STRIDE, LANES), jnp.int32)],
        compiler_params=_params(("arbitrary",)),
        name="peer_act",
    )(idx_t, h2, gate_t, tab)


def _peer_out_kernel(idx_ref, c_ref, tab_vmem, o_ref):
    tb = o_ref.shape[0]
    R = TABLE_ROWS_PER_EXPERT
    nacc = 4

    def token(t, carry):
        lo_acc = [jnp.zeros((R, LANES), F32) for _ in range(nacc)]
        hi_acc = [jnp.zeros((R, LANES), F32) for _ in range(nacc)]
        for k in range(PEER_SEL):
            e = idx_ref[k, t]
            c = c_ref[k, t]
            lo, hi = _unpack(tab_vmem[pl.ds(pl.multiple_of(e * R, R), R), :])
            lo_acc[k % nacc] = lo_acc[k % nacc] + lo * c
            hi_acc[k % nacc] = hi_acc[k % nacc] + hi * c
        o_ref[t, 0:R, :] = (lo_acc[0] + lo_acc[1]) + (lo_acc[2] + lo_acc[3])
        o_ref[t, R:2 * R, :] = (hi_acc[0] + hi_acc[1]) + (hi_acc[2] + hi_acc[3])
        return carry

    lax.fori_loop(0, tb, token, 0)


def _peer_out(idx_t, c_t, tab):
    t = idx_t.shape[1]
    tb = PEER_GATHER_TOKENS
    rows = 2 * TABLE_ROWS_PER_EXPERT
    return pl.pallas_call(
        _peer_out_kernel,
        grid=(t // tb,),
        in_specs=[pl.BlockSpec((PEER_SEL, tb), lambda i: (0, i), memory_space=pltpu.SMEM),
                  pl.BlockSpec((PEER_SEL, tb), lambda i: (0, i), memory_space=pltpu.SMEM),
                  _table_spec(tab)],
        out_specs=pl.BlockSpec((tb, rows, LANES), lambda i: (i, 0, 0)),
        out_shape=jax.ShapeDtypeStruct((t, rows, LANES), F32),
        compiler_params=_params(("arbitrary",)),
        name="peer_out",
    )(idx_t, c_t, tab)


def _mixer_layer(x2, bsz, s, p, rope, ret_tables):
    ua, mb, rt = _in_proj(x2, p['norm1_g'], p['w_in'], p['w_in_qk_lo'])
    ya = _s5_mixer(ua, bsz, s, p['s5_bcat'], p['s5_ccat'], p['s5_tabs'], p['s5_d'], p['s5_w_glu'])
    qT, k, vT, gate = _moba_prep(mb, bsz, s, p['moba_q_g'], p['moba_k_g'], *rope)
    ot = _moba_attn(qT, k, vT, gate)
    yc = _ret_mixer(rt, bsz, s, ret_tables)
    return _out_proj(x2, ya, ot, yc, p['mix_out_g'], p['w_out'], bsz, s)


def _peer_layer(x2, p):
    h2, idx_t, gate_t = _peer_route(x2, p['norm2_g'], p['peer_wqT'], p['peer_sk'])
    c_t = _peer_act(idx_t, h2, gate_t, p['peer_u'])
    out = _peer_out(idx_t, c_t, p['peer_v'])
    return x2 + out.reshape(x2.shape)


def kernel(x, norm1_g, w_in, s5_lam_re, s5_lam_im, s5_log_dt, s5_b_re, s5_b_im, s5_c_re, s5_c_im,
           s5_d, s5_w_glu, moba_q_g, moba_k_g, mix_out_g, w_out, norm2_g, peer_w_q, peer_sub_keys,
           peer_u, peer_v):
    bsz, s, d = x.shape
    depth = w_in.shape[0]
    pos = jnp.arange(s, dtype=F32)
    inv = ROPE_THETA ** (-jnp.arange(0, ROPE_DIMS, 2, dtype=F32) / ROPE_DIMS)
    ang = (pos[:, None] * inv[None, :]).T
    rope = (jnp.cos(ang), jnp.sin(ang))
    ret_tables = _ret_tables(s)
    x2 = x.reshape(bsz * s, d)
    for l in range(depth):
        bcat, ccat, tabs = _s5_operands(s5_lam_re[l], s5_lam_im[l], s5_log_dt[l], s5_b_re[l],
                                        s5_b_im[l], s5_c_re[l], s5_c_im[l])
        w_in_hi = w_in[l].astype(BF16)
        qk_cols = slice(S5_CH, S5_CH + 2 * MOBA_W)
        p = {
            'norm1_g': norm1_g[l][None, :],
            'w_in': w_in_hi,
            'w_in_qk_lo': (w_in[l][:, qk_cols] - w_in_hi[:, qk_cols].astype(F32)).astype(BF16),
            's5_bcat': bcat, 's5_ccat': ccat, 's5_tabs': tabs,
            's5_d': s5_d[l][None, :],
            's5_w_glu': s5_w_glu[l].astype(BF16),
            'moba_q_g': moba_q_g[l][:, None],
            'moba_k_g': moba_k_g[l][:, None],
            'mix_out_g': mix_out_g[l][None, :],
            'w_out': w_out[l].astype(BF16),
            'norm2_g': norm2_g[l][None, :],
            'peer_wqT': peer_w_q[l].T.astype(BF16),
            'peer_sk': peer_sub_keys[l].reshape(2 * PEER_HEADS, PEER_KEYS, PEER_QDIM // 2).astype(BF16),
            'peer_u': _pack_table(peer_u[l]),
            'peer_v': _pack_table(peer_v[l]),
        }
        x2 = _mixer_layer(x2, bsz, s, p, rope, ret_tables)
        x2 = _peer_layer(x2, p)
    return x2.reshape(bsz, s, d)
```

```python
import functools
import math

import numpy as np
import jax
import jax.numpy as jnp
from jax import lax
from jax.experimental import pallas as pl
from jax.experimental.pallas import tpu as pltpu

D_MODEL = 1024
HEAD_DIM = 64
S5_CH = 256
S5_GROUP = 16
S5_NGROUPS = 16
S5_STATE = 64
S5_W = S5_NGROUPS * S5_STATE
MOBA_HEADS = 8
MOBA_W = 512
MOBA_BLOCK = 256
MOBA_TOPK = 3
ROPE_THETA = 500000.0
ROPE_DIMS = 16
RET_HEADS = 4
RET_W = 256
RET_ANGLE_BASE = 10000.0
MIX_W = S5_CH + MOBA_W + RET_W
IN_W = S5_CH + 3 * MOBA_W + 4 * RET_W
PEER_KEYS = 128
PEER_HEADS = 8
PEER_TOPK = 16
PEER_QDIM = 128
PEER_SEL = PEER_HEADS * PEER_TOPK
EPS = 1e-6

LANES = 128
TABLE_ROWS_PER_EXPERT = D_MODEL // (2 * LANES)
TILE_STRIDE = 136
VMEM_LIMIT = 56 * 1024 * 1024

S5_CHUNK = 128
RET_CHUNK = 256
PROJ_ROWS = 256
PEER_ROUTE_TOKENS = 256
PEER_GATHER_TOKENS = 128

BF16 = jnp.bfloat16
F32 = jnp.float32
NEG_INF = float("-inf")


def _params(sem):
    return pltpu.CompilerParams(dimension_semantics=sem, vmem_limit_bytes=VMEM_LIMIT)


def _gelu_tanh(y):
    return 0.5 * y * (1.0 + jnp.tanh(0.7978845608028654 * (y + 0.044715 * (y * y * y))))


def _rms_scale(x, axis):
    return lax.rsqrt(jnp.mean(x * x, axis=axis, keepdims=True) + EPS)


def _split(a):
    hi = a.astype(BF16)
    return hi, (a - hi.astype(F32)).astype(BF16)


def _dot3(a, b):
    a_hi, a_lo = _split(a)
    b_hi, b_lo = _split(b)
    out = jnp.dot(a_hi, b_hi, preferred_element_type=F32)
    out = out + jnp.dot(a_hi, b_lo, preferred_element_type=F32)
    return out + jnp.dot(a_lo, b_hi, preferred_element_type=F32)


def _in_proj_kernel(x_ref, g_ref, w_ref, wlo_ref, ua_ref, mb_ref, rt_ref):
    x = x_ref[...]
    hf = x * _rms_scale(x, -1) * g_ref[...]
    h = hf.astype(BF16)
    ua_ref[...] = jnp.dot(h, w_ref[:, 0:S5_CH], preferred_element_type=F32)
    h_lo = (hf - h.astype(F32)).astype(BF16)
    q0, q1 = S5_CH, S5_CH + 2 * MOBA_W
    qk = jnp.dot(h, w_ref[:, q0:q1], preferred_element_type=F32)
    qk = qk + jnp.dot(h, wlo_ref[...], preferred_element_type=F32)
    qk = qk + jnp.dot(h_lo, w_ref[:, q0:q1], preferred_element_type=F32)
    mb_ref[:, 0:2 * MOBA_W] = qk
    mb_ref[:, 2 * MOBA_W:3 * MOBA_W] = jnp.dot(h, w_ref[:, q1:q1 + MOBA_W], preferred_element_type=F32)
    rt_ref[...] = jnp.dot(h, w_ref[:, S5_CH + 3 * MOBA_W:IN_W], preferred_element_type=F32)


def _in_proj(x2, g, w, wlo):
    t = x2.shape[0]
    tm = PROJ_ROWS
    return pl.pallas_call(
        _in_proj_kernel,
        grid=(t // tm,),
        in_specs=[pl.BlockSpec((tm, D_MODEL), lambda i: (i, 0)),
                  pl.BlockSpec((1, D_MODEL), lambda i: (0, 0)),
                  pl.BlockSpec((D_MODEL, IN_W), lambda i: (0, 0)),
                  pl.BlockSpec((D_MODEL, 2 * MOBA_W), lambda i: (0, 0))],
        out_specs=[pl.BlockSpec((tm, S5_CH), lambda i: (i, 0)),
                   pl.BlockSpec((tm, 3 * MOBA_W), lambda i: (i, 0)),
                   pl.BlockSpec((tm, 4 * RET_W), lambda i: (i, 0))],
        out_shape=[jax.ShapeDtypeStruct((t, S5_CH), F32),
                   jax.ShapeDtypeStruct((t, 3 * MOBA_W), F32),
                   jax.ShapeDtypeStruct((t, 4 * RET_W), F32)],
        compiler_params=_params(("parallel",)),
        name="in_proj",
    )(x2, g, w, wlo)


def _s5_kernel(u_ref, bcat_ref, ccat_ref, tab_ref, d_ref, wglu_ref, o_ref, st_ref):
    L = S5_CHUNK

    @pl.when(pl.program_id(1) == 0)
    def _():
        st_ref[...] = jnp.zeros_like(st_ref)

    u = u_ref[...]
    bu = jnp.dot(u.astype(BF16), bcat_ref[...], preferred_element_type=F32)
    bur, bui = bu[:, :S5_W], bu[:, S5_W:]
    air, aii = tab_ref[0], tab_ref[1]
    ktr = bur * air - bui * aii
    kti = bur * aii + bui * air
    row = lax.broadcasted_iota(jnp.int32, (L, L), 0)
    col = lax.broadcasted_iota(jnp.int32, (L, L), 1)
    tri = jnp.where(row >= col, 1.0, 0.0).astype(BF16)
    cr = jnp.dot(tri, ktr.astype(BF16), preferred_element_type=F32)
    ci = jnp.dot(tri, kti.astype(BF16), preferred_element_type=F32)
    apr, api = tab_ref[2], tab_ref[3]
    acr, aci = tab_ref[4], tab_ref[5]
    pr = st_ref[0:1, :]
    pi = st_ref[1:2, :]
    sr = apr * cr - api * ci + acr * pr - aci * pi
    si = apr * ci + api * cr + acr * pi + aci * pr
    st_ref[0:1, :] = sr[L - 1:L, :]
    st_ref[1:2, :] = si[L - 1:L, :]
    y = (jnp.dot(sr.astype(BF16), ccat_ref[0:S5_W, :], preferred_element_type=F32)
         + jnp.dot(si.astype(BF16), ccat_ref[S5_W:2 * S5_W, :], preferred_element_type=F32))
    y = _gelu_tanh(y + u * d_ref[...])
    z = jnp.dot(y.astype(BF16), wglu_ref[...], preferred_element_type=F32)
    y = y * jax.nn.sigmoid(z)
    o_ref[...] = y * _rms_scale(y, -1)


def _s5_operands(lam_re, lam_im, log_dt, b_re, b_im, c_re, c_im):
    dt = jnp.exp(log_dt)[:, None]
    mag = jnp.exp(lam_re * dt)
    ar = mag * jnp.cos(lam_im * dt)
    ai = mag * jnp.sin(lam_im * dt)
    den = lam_re * lam_re + lam_im * lam_im
    fr = ((ar - 1.0) * lam_re + ai * lam_im) / den
    fi = (ai * lam_re - (ar - 1.0) * lam_im) / den
    bbr = fr[..., None] * b_re - fi[..., None] * b_im
    bbi = fr[..., None] * b_im + fi[..., None] * b_re
    eye = jnp.eye(S5_NGROUPS, dtype=F32)
    blk = lambda m: jnp.einsum('gph,gk->ghkp', m, eye).reshape(S5_CH, S5_W)
    bcat = jnp.concatenate([blk(bbr), blk(bbi)], axis=1).astype(BF16)
    blc = lambda m: jnp.einsum('ghp,gk->gpkh', m, eye).reshape(S5_W, S5_CH)
    ccat = jnp.concatenate([blc(c_re), -blc(c_im)], axis=0).astype(BF16)

    def powers(n):
        lr = (lam_re * dt).reshape(1, S5_W)
        li = (lam_im * dt).reshape(1, S5_W)
        m = jnp.exp(lr * n[:, None])
        return m * jnp.cos(li * n[:, None]), m * jnp.sin(li * n[:, None])

    j = jnp.arange(S5_CHUNK, dtype=F32)
    tabs = jnp.stack([*powers(-j), *powers(j), *powers(j + 1.0)], axis=0)
    return bcat, ccat, tabs


def _s5_mixer(ua, bsz, s, bcat, ccat, tabs, d_skip, w_glu):
    L = S5_CHUNK
    nc = s // L
    return pl.pallas_call(
        _s5_kernel,
        grid=(bsz, nc),
        in_specs=[pl.BlockSpec((L, S5_CH), lambda b, j: (b * nc + j, 0)),
                  pl.BlockSpec((S5_CH, 2 * S5_W), lambda b, j: (0, 0)),
                  pl.BlockSpec((2 * S5_W, S5_CH), lambda b, j: (0, 0)),
                  pl.BlockSpec((6, L, S5_W), lambda b, j: (0, 0, 0)),
                  pl.BlockSpec((1, S5_CH), lambda b, j: (0, 0)),
                  pl.BlockSpec((S5_CH, S5_CH), lambda b, j: (0, 0))],
        out_specs=pl.BlockSpec((L, S5_CH), lambda b, j: (b * nc + j, 0)),
        out_shape=jax.ShapeDtypeStruct((bsz * s, S5_CH), F32),
        scratch_shapes=[pltpu.VMEM((2, S5_W), F32)],
        compiler_params=_params(("parallel", "arbitrary")),
        name="s5_mixer",
    )(ua, bcat, ccat, tabs, d_skip, w_glu)


def _moba_prep_kernel(qkv_ref, gq_ref, gk_ref, cos_ref, sin_ref, qT_ref, k_ref, vT_ref, gate_ref, km_ref):
    j = pl.program_id(1)

    @pl.when(j == 0)
    def _():
        km_ref[...] = jnp.zeros_like(km_ref)

    x = qkv_ref[...]
    c = cos_ref[...]
    s = sin_ref[...]
    half = ROPE_DIMS // 2

    def prep(xT, g):
        y = xT * _rms_scale(xT, 0) * g
        x1 = y[0:half]
        x2 = y[half:ROPE_DIMS]
        return jnp.concatenate([x1 * c - x2 * s, x2 * c + x1 * s, y[ROPE_DIMS:]], axis=0)

    qT_all = x[:, 0:MOBA_W].T
    kT_all = x[:, MOBA_W:2 * MOBA_W].T
    vT_all = x[:, 2 * MOBA_W:3 * MOBA_W].T
    gq = gq_ref[...]
    gk = gk_ref[...]
    for h in range(MOBA_HEADS):
        sl = slice(h * HEAD_DIM, (h + 1) * HEAD_DIM)
        q = prep(qT_all[sl], gq) * (HEAD_DIM ** -0.5)
        qT_ref[h] = q.astype(BF16)
        gate_ref[h] = _dot3(km_ref[h], q)
        k = prep(kT_all[sl], gk).T
        k_ref[h] = k.astype(BF16)
        km_ref[h, pl.ds(j, 1), :] = jnp.mean(k, axis=0, keepdims=True)
        vT_ref[h] = vT_all[sl].astype(BF16)


def _moba_prep(qkv, bsz, s, gq, gk, cosT, sinT):
    nb = s // MOBA_BLOCK
    H, dh, blk = MOBA_HEADS, HEAD_DIM, MOBA_BLOCK
    return pl.pallas_call(
        _moba_prep_kernel,
        grid=(bsz, nb),
        in_specs=[pl.BlockSpec((blk, 3 * MOBA_W), lambda b, j: (b * nb + j, 0)),
                  pl.BlockSpec((dh, 1), lambda b, j: (0, 0)),
                  pl.BlockSpec((dh, 1), lambda b, j: (0, 0)),
                  pl.BlockSpec((ROPE_DIMS // 2, blk), lambda b, j: (0, j)),
                  pl.BlockSpec((ROPE_DIMS // 2, blk), lambda b, j: (0, j))],
        out_specs=[pl.BlockSpec((None, H, dh, blk), lambda b, j: (b, 0, 0, j)),
                   pl.BlockSpec((None, H, None, blk, dh), lambda b, j: (b, 0, j, 0, 0)),
                   pl.BlockSpec((None, H, None, dh, blk), lambda b, j: (b, 0, j, 0, 0)),
                   pl.BlockSpec((None, H, nb, blk), lambda b, j: (b, 0, 0, j))],
        out_shape=[jax.ShapeDtypeStruct((bsz, H, dh, s), BF16),
                   jax.ShapeDtypeStruct((bsz, H, nb, blk, dh), BF16),
                   jax.ShapeDtypeStruct((bsz, H, nb, dh, blk), BF16),
                   jax.ShapeDtypeStruct((bsz, H, nb, s), F32)],
        scratch_shapes=[pltpu.VMEM((H, nb, dh), F32)],
        compiler_params=_params(("parallel", "arbitrary")),
        name="moba_prep",
    )(qkv, gq, gk, cosT, sinT)


def _moba_attn_kernel(qT_ref, k_ref, vT_ref, gate_ref, o_ref, msk_ref):
    i = pl.program_id(2)
    nb = gate_ref.shape[0]
    blk = MOBA_BLOCK
    qT = qT_ref[...]
    gate = gate_ref[...]
    row = lax.broadcasted_iota(jnp.int32, (nb, blk), 0)
    gm = jnp.where(row < i, gate, NEG_INF)
    cnt = jnp.zeros((nb, blk), jnp.int32)
    for m in range(nb):
        gmm = gm[m:m + 1, :]
        beats = jnp.where(gmm > gm, 1, jnp.where(gmm == gm, jnp.where(row > m, 1, 0), 0))
        cnt = cnt + beats
    sel = jnp.where(row < i, jnp.where(cnt < MOBA_TOPK, 0.0, NEG_INF), NEG_INF)
    msk_ref[...] = sel

    kpos = lax.broadcasted_iota(jnp.int32, (blk, blk), 0)
    qpos = lax.broadcasted_iota(jnp.int32, (blk, blk), 1)
    sT = jnp.dot(k_ref[i], qT, preferred_element_type=F32)
    sT = jnp.where(kpos <= qpos, sT, NEG_INF)
    m0 = jnp.max(sT, axis=0, keepdims=True)
    p = jnp.exp(sT - m0)
    l0 = jnp.sum(p, axis=0, keepdims=True)
    acc0 = jnp.dot(vT_ref[i], p.astype(BF16), preferred_element_type=F32)

    def body(n, carry):
        m, l, acc = carry
        sT = jnp.dot(k_ref[n], qT, preferred_element_type=F32) + msk_ref[pl.ds(n, 1), :]
        m_new = jnp.maximum(m, jnp.max(sT, axis=0, keepdims=True))
        alpha = jnp.exp(m - m_new)
        p = jnp.exp(sT - m_new)
        l = alpha * l + jnp.sum(p, axis=0, keepdims=True)
        acc = alpha * acc + jnp.dot(vT_ref[n], p.astype(BF16), preferred_element_type=F32)
        return m_new, l, acc

    _, l, acc = lax.fori_loop(0, i, body, (m0, l0, acc0))
    o_ref[...] = acc / l


def _moba_attn(qT, k, vT, gate):
    bsz, H, dh, s = qT.shape
    nb = s // MOBA_BLOCK
    blk = MOBA_BLOCK
    return pl.pallas_call(
        _moba_attn_kernel,
        grid=(bsz, H, nb),
        in_specs=[pl.BlockSpec((None, None, dh, blk), lambda b, h, i: (b, h, 0, i)),
                  pl.BlockSpec((None, None, nb, blk, dh), lambda b, h, i: (b, h, 0, 0, 0)),
                  pl.BlockSpec((None, None, nb, dh, blk), lambda b, h, i: (b, h, 0, 0, 0)),
                  pl.BlockSpec((None, None, nb, blk), lambda b, h, i: (b, h, 0, i))],
        out_specs=pl.BlockSpec((None, None, dh, blk), lambda b, h, i: (b, h, 0, i)),
        out_shape=jax.ShapeDtypeStruct((bsz, H, dh, s), F32),
        scratch_shapes=[pltpu.VMEM((nb, blk), F32)],
        compiler_params=_params(("parallel", "parallel", "arbitrary")),
        name="moba_attn",
    )(qT, k, vT, gate)


def _ret_log_decay():
    return np.log(1.0 - 2.0 ** (-5.0 - np.arange(RET_HEADS, dtype=np.float64)))


def _ret_kernel(x_ref, cos_ref, sin_ref, xi_ref, zeta_ref, dm_ref, o_ref, st_ref):
    C = RET_CHUNK

    @pl.when(pl.program_id(1) == 0)
    def _():
        st_ref[...] = jnp.zeros_like(st_ref)

    x = x_ref[...]
    q = x[:, 0:RET_W]
    k = x[:, RET_W:2 * RET_W]
    v = x[:, 2 * RET_W:3 * RET_W]
    g = x[:, 3 * RET_W:4 * RET_W]
    cos = cos_ref[...]
    sin = sin_ref[...]
    lane = lax.broadcasted_iota(jnp.int32, (C, RET_W), 1)
    first = (lane & (HEAD_DIM // 2)) == 0

    def rot(t):
        swapped = jnp.where(first, pltpu.roll(t, RET_W - HEAD_DIM // 2, 1), pltpu.roll(t, HEAD_DIM // 2, 1))
        return t * cos + swapped * sin

    qr = rot(q)
    kr = rot(k) * (HEAD_DIM ** -0.5)
    qb = qr.astype(BF16)
    kb = kr.astype(BF16)
    vb = v.astype(BF16)
    qx = (qr * xi_ref[...]).astype(BF16)
    kz = kr * zeta_ref[...]
    decay_c = np.exp(C * _ret_log_decay())
    for h in range(RET_HEADS):
        sl = slice(h * HEAD_DIM, (h + 1) * HEAD_DIM)
        sc = lax.dot_general(qb[:, sl], kb[:, sl], (((1,), (1,)), ((), ())),
                             preferred_element_type=F32) * dm_ref[h]
        inner = jnp.dot(sc.astype(BF16), vb[:, sl], preferred_element_type=F32)
        r_prev = st_ref[h]
        cross = jnp.dot(qx[:, sl], r_prev.astype(BF16), preferred_element_type=F32)
        kv = jnp.dot(kz[:, sl].T.astype(BF16), vb[:, sl], preferred_element_type=F32)
        st_ref[h] = float(decay_c[h]) * r_prev + kv
        o = inner + cross
        o = o * _rms_scale(o, -1)
        gh = g[:, sl]
        o_ref[:, sl] = gh * jax.nn.sigmoid(gh) * o


def _ret_tables(s):
    C = RET_CHUNK
    half = HEAD_DIM // 2
    pos = jnp.arange(s, dtype=F32)
    inv = RET_ANGLE_BASE ** (-jnp.linspace(0.0, 1.0, half, dtype=F32))
    ang = pos[:, None] * inv[None, :]
    cos, sin = jnp.cos(ang), jnp.sin(ang)
    cos_t = jnp.tile(cos, (1, 2 * RET_HEADS))
    sin_t = jnp.tile(jnp.concatenate([-sin, sin], axis=1), (1, RET_HEADS))
    lg = _ret_log_decay()
    i = np.arange(C, dtype=np.float64)
    xi = np.repeat(np.exp((i + 1.0)[:, None] * lg[None, :]), HEAD_DIM, axis=1)
    zeta = np.repeat(np.exp((C - 1.0 - i)[:, None] * lg[None, :]), HEAD_DIM, axis=1)
    diff = i[:, None] - i[None, :]
    dm = np.where(diff >= 0, np.exp(np.maximum(diff, 0.0)[None] * lg[:, None, None]), 0.0)
    return cos_t, sin_t, jnp.asarray(xi, F32), jnp.asarray(zeta, F32), jnp.asarray(dm, F32)


def _ret_mixer(rt, bsz, s, tables):
    C = RET_CHUNK
    nc = s // C
    cos_t, sin_t, xi, zeta, dm = tables
    return pl.pallas_call(
        _ret_kernel,
        grid=(bsz, nc),
        in_specs=[pl.BlockSpec((C, 4 * RET_W), lambda b, j: (b * nc + j, 0)),
                  pl.BlockSpec((C, RET_W), lambda b, j: (j, 0)),
                  pl.BlockSpec((C, RET_W), lambda b, j: (j, 0)),
                  pl.BlockSpec((C, RET_W), lambda b, j: (0, 0)),
                  pl.BlockSpec((C, RET_W), lambda b, j: (0, 0)),
                  pl.BlockSpec((RET_HEADS, C, C), lambda b, j: (0, 0, 0))],
        out_specs=pl.BlockSpec((C, RET_W), lambda b, j: (b * nc + j, 0)),
        out_shape=jax.ShapeDtypeStruct((bsz * s, RET_W), F32),
        scratch_shapes=[pltpu.VMEM((RET_HEADS, HEAD_DIM, HEAD_DIM), F32)],
        compiler_params=_params(("parallel", "arbitrary")),
        name="ret_mixer",
    )(rt, cos_t, sin_t, xi, zeta, dm)


def _out_proj_kernel(x_ref, ya_ref, ot_ref, yc_ref, g_ref, w_ref, o_ref):
    ts = x_ref.shape[0]
    ot = ot_ref[...]
    yb = (ot * _rms_scale(ot, 1)).reshape(MOBA_W, ts).T
    g = g_ref[...]
    a0, a1 = S5_CH, S5_CH + MOBA_W
    acc = x_ref[...]
    acc = acc + jnp.dot((ya_ref[...] * g[:, 0:a0]).astype(BF16), w_ref[0:a0, :], preferred_element_type=F32)
    acc = acc + jnp.dot((yb * g[:, a0:a1]).astype(BF16), w_ref[a0:a1, :], preferred_element_type=F32)
    acc = acc + jnp.dot((yc_ref[...] * g[:, a1:MIX_W]).astype(BF16), w_ref[a1:MIX_W, :],
                        preferred_element_type=F32)
    o_ref[...] = acc


def _out_proj(x2, ya, ot, yc, g, w, bsz, s):
    ts = PROJ_ROWS
    nt = s // ts
    return pl.pallas_call(
        _out_proj_kernel,
        grid=(bsz, nt),
        in_specs=[pl.BlockSpec((ts, D_MODEL), lambda b, j: (b * nt + j, 0)),
                  pl.BlockSpec((ts, S5_CH), lambda b, j: (b * nt + j, 0)),
                  pl.BlockSpec((None, MOBA_HEADS, HEAD_DIM, ts), lambda b, j: (b, 0, 0, j)),
                  pl.BlockSpec((ts, RET_W), lambda b, j: (b * nt + j, 0)),
                  pl.BlockSpec((1, MIX_W), lambda b, j: (0, 0)),
                  pl.BlockSpec((MIX_W, D_MODEL), lambda b, j: (0, 0))],
        out_specs=pl.BlockSpec((ts, D_MODEL), lambda b, j: (b * nt + j, 0)),
        out_shape=jax.ShapeDtypeStruct((bsz * s, D_MODEL), F32),
        compiler_params=_params(("parallel", "parallel")),
        name="out_proj",
    )(x2, ya, ot, yc, g, w)


def _top16(s, payload):
    n = s.shape[0]
    io = lax.broadcasted_iota(jnp.int32, s.shape, 0)
    vals, pays = [], []
    for _ in range(PEER_TOPK):
        m = jnp.max(s, axis=0, keepdims=True)
        pos = jnp.min(jnp.where(s == m, io, n), axis=0, keepdims=True)
        hit = io == pos
        vals.append(m)
        pays.append(jnp.sum(jnp.where(hit, payload, 0), axis=0, keepdims=True))
        s = jnp.where(hit, NEG_INF, s)
    return jnp.concatenate(vals, axis=0), jnp.concatenate(pays, axis=0)


def _peer_route_kernel(x_ref, g_ref, wqT_ref, sk_ref, h2_ref, idx_ref, gate_ref, q_scr):
    tb = x_ref.shape[0]
    x = x_ref[...]
    h2 = x * _rms_scale(x, -1) * g_ref[...]
    h2_ref[...] = h2
    q_scr[...] = lax.dot_general(wqT_ref[...], h2.astype(BF16), (((1,), (1,)), ((), ())),
                                 preferred_element_type=F32).astype(BF16)
    half = PEER_QDIM // 2
    io_keys = lax.broadcasted_iota(jnp.int32, (PEER_KEYS, tb), 0)

    def head(h, carry):
        sub = []
        for c in range(2):
            grp = 2 * h + c
            qg = q_scr[pl.ds(pl.multiple_of(grp * half, half), half), :]
            sc = jnp.dot(sk_ref[grp], qg, preferred_element_type=F32)
            sub.append(_top16(sc, io_keys))
        (v0, i0), (v1, i1) = sub
        cand = jnp.concatenate([v0[a:a + 1, :] + v1 for a in range(PEER_TOPK)], axis=0)
        cid = jnp.concatenate([i0[a:a + 1, :] * PEER_KEYS + i1 for a in range(PEER_TOPK)], axis=0)
        fs, eidx = _top16(cand, cid)
        e = jnp.exp(fs - fs[0:1, :])
        gates = e / jnp.sum(e, axis=0, keepdims=True)
        r0 = pl.multiple_of(h * PEER_TOPK, PEER_TOPK)
        idx_ref[pl.ds(r0, PEER_TOPK), :] = eidx
        gate_ref[pl.ds(r0, PEER_TOPK), :] = gates
        return carry

    lax.fori_loop(0, PEER_HEADS, head, 0)


def _peer_route(x2, g, wqT, sk):
    t = x2.shape[0]
    tb = PEER_ROUTE_TOKENS
    ngrp = 2 * PEER_HEADS
    return pl.pallas_call(
        _peer_route_kernel,
        grid=(t // tb,),
        in_specs=[pl.BlockSpec((tb, D_MODEL), lambda i: (i, 0)),
                  pl.BlockSpec((1, D_MODEL), lambda i: (0, 0)),
                  pl.BlockSpec((PEER_HEADS * PEER_QDIM, D_MODEL), lambda i: (0, 0)),
                  pl.BlockSpec((ngrp, PEER_KEYS, PEER_QDIM // 2), lambda i: (0, 0, 0))],
        out_specs=[pl.BlockSpec((tb, D_MODEL), lambda i: (i, 0)),
                   pl.BlockSpec((PEER_SEL, tb), lambda i: (0, i)),
                   pl.BlockSpec((PEER_SEL, tb), lambda i: (0, i))],
        out_shape=[jax.ShapeDtypeStruct((t, D_MODEL), F32),
                   jax.ShapeDtypeStruct((PEER_SEL, t), jnp.int32),
                   jax.ShapeDtypeStruct((PEER_SEL, t), F32)],
        scratch_shapes=[pltpu.VMEM((PEER_HEADS * PEER_QDIM, tb), BF16)],
        compiler_params=_params(("parallel",)),
        name="peer_route",
    )(x2, g, wqT, sk)


def _pack_table(tab):
    bits = lax.bitcast_convert_type(tab.astype(BF16), jnp.uint16).astype(jnp.uint32)
    half = D_MODEL // 2
    words = bits[:, :half] | (bits[:, half:] << 16)
    return lax.bitcast_convert_type(words, jnp.int32).reshape(tab.shape[0] * TABLE_ROWS_PER_EXPERT, LANES)


def _table_spec(tab):
    return pl.BlockSpec(tab.shape, lambda i: (0, 0), pipeline_mode=pl.Buffered(1))


def _gather_rows(idx_ref, t, tab_ref, tile_ref):
    R = TABLE_ROWS_PER_EXPERT
    for k in range(PEER_SEL):
        row = pl.multiple_of(idx_ref[t, k], R)
        tile_ref[pl.ds(k, R, stride=TILE_STRIDE), :] = tab_ref[pl.ds(row, R), :]


def _tile_chunk(tile_ref, r):
    return pltpu.bitcast(tile_ref[r * TILE_STRIDE:r * TILE_STRIDE + PEER_SEL, :], BF16)


def _pipelined_tokens(tb, idx_ref, tab_ref, tiles, compute):
    tile_a, tile_b = tiles
    _gather_rows(idx_ref, 0, tab_ref, tile_a)

    def pair(i, carry):
        t0 = 2 * i
        _gather_rows(idx_ref, t0 + 1, tab_ref, tile_b)
        compute(t0, tile_a)
        _gather_rows(idx_ref, jnp.minimum(t0 + 2, tb - 1), tab_ref, tile_a)
        compute(t0 + 1, tile_b)
        return carry

    lax.fori_loop(0, tb // 2, pair, 0)


def _hi_lo_rows(v, lane_parity):
    n = v.shape[1]
    vb = jnp.broadcast_to(v, (8, n))
    hi = vb.astype(BF16).astype(F32)
    srow = lax.broadcasted_iota(jnp.int32, (8, n), 0)
    part = jnp.where(srow < 2, hi, vb - hi)
    part = jnp.where(srow < 4, part, 0.0)
    return jnp.where((srow & 1) == lane_parity, part, 0.0)


def _peer_act_kernel(idx_ref, x_ref, gate_ref, tab_ref, c_ref, tile_a, tile_b):
    tb = x_ref.shape[0]
    R = TABLE_ROWS_PER_EXPERT
    half = D_MODEL // 2
    lane = lax.broadcasted_iota(jnp.int32, (1, 2 * PEER_SEL), 1)
    even = (lane & 1) == 0

    def compute(t, tile_ref):
        x = x_ref[pl.ds(t, 1), :]
        xb = jnp.broadcast_to(x, (8, D_MODEL))
        hi = xb.astype(BF16).astype(F32)
        part = jnp.where(lax.broadcasted_iota(jnp.int32, (8, D_MODEL), 0) < 2, hi, xb - hi)
        srow = lax.broadcasted_iota(jnp.int32, (8, half), 0)
        rows = jnp.where((srow & 1) == 0, part[:, :half], part[:, half:])
        rows = jnp.where(srow < 4, rows, 0.0).astype(BF16)
        acc = jnp.zeros((8, 2 * PEER_SEL), F32)
        for r in range(R):
            acc = acc + lax.dot_general(rows[:, r * LANES:(r + 1) * LANES], _tile_chunk(tile_ref, r),
                                        (((1,), (1,)), ((), ())), preferred_element_type=F32)
        a = jnp.where(even, acc[0:1] + acc[2:3], acc[1:2] + acc[3:4])
        a = a + jnp.where(even, pltpu.roll(a, 2 * PEER_SEL - 1, 1), pltpu.roll(a, 1, 1))
        c_ref[t] = a

    _pipelined_tokens(tb, idx_ref, tab_ref, (tile_a, tile_b), compute)
    c_ref[...] = gate_ref[...] * _gelu_tanh(c_ref[...])


def _tile_scratch():
    shape = (TABLE_ROWS_PER_EXPERT * TILE_STRIDE, LANES)
    return [pltpu.VMEM(shape, jnp.int32), pltpu.VMEM(shape, jnp.int32)]


def _peer_act(idx, h2, gate_i, tab):
    t = h2.shape[0]
    tb = PEER_GATHER_TOKENS
    return pl.pallas_call(
        _peer_act_kernel,
        grid=(t // tb,),
        in_specs=[pl.BlockSpec((tb, PEER_SEL), lambda i: (i, 0), memory_space=pltpu.SMEM),
                  pl.BlockSpec((tb, D_MODEL), lambda i: (i, 0)),
                  pl.BlockSpec((tb, 1, 2 * PEER_SEL), lambda i: (i, 0, 0)),
                  _table_spec(tab)],
        out_specs=pl.BlockSpec((tb, 1, 2 * PEER_SEL), lambda i: (i, 0, 0)),
        out_shape=jax.ShapeDtypeStruct((t, 1, 2 * PEER_SEL), F32),
        scratch_shapes=_tile_scratch(),
        compiler_params=_params(("arbitrary",)),
        name="peer_act",
    )(idx, h2, gate_i, tab)


def _peer_out_kernel(idx_ref, c_ref, tab_ref, o_ref, tile_a, tile_b):
    tb = o_ref.shape[0]
    R = TABLE_ROWS_PER_EXPERT
    half = D_MODEL // 2
    lane_parity = lax.broadcasted_iota(jnp.int32, (8, 2 * PEER_SEL), 1) & 1

    def compute(t, tile_ref):
        rows = _hi_lo_rows(c_ref[t], lane_parity).astype(BF16)
        for r in range(R):
            res = jnp.dot(rows, _tile_chunk(tile_ref, r), preferred_element_type=F32)
            o_ref[t, :, r * LANES:(r + 1) * LANES] = res[0:1] + res[2:3]
            o_ref[t, :, half + r * LANES:half + (r + 1) * LANES] = res[1:2] + res[3:4]

    _pipelined_tokens(tb, idx_ref, tab_ref, (tile_a, tile_b), compute)


def _peer_out(idx, c_i, tab):
    t = idx.shape[0]
    tb = PEER_GATHER_TOKENS
    return pl.pallas_call(
        _peer_out_kernel,
        grid=(t // tb,),
        in_specs=[pl.BlockSpec((tb, PEER_SEL), lambda i: (i, 0), memory_space=pltpu.SMEM),
                  pl.BlockSpec((tb, 1, 2 * PEER_SEL), lambda i: (i, 0, 0)),
                  _table_spec(tab)],
        out_specs=pl.BlockSpec((tb, 1, D_MODEL), lambda i: (i, 0, 0)),
        out_shape=jax.ShapeDtypeStruct((t, 1, D_MODEL), F32),
        scratch_shapes=_tile_scratch(),
        compiler_params=_params(("arbitrary",)),
        name="peer_out",
    )(idx, c_i, tab)


def _mixer_layer(x2, bsz, s, p, rope, ret_tables):
    ua, mb, rt = _in_proj(x2, p['norm1_g'], p['w_in'], p['w_in_qk_lo'])
    ya = _s5_mixer(ua, bsz, s, p['s5_bcat'], p['s5_ccat'], p['s5_tabs'], p['s5_d'], p['s5_w_glu'])
    qT, k, vT, gate = _moba_prep(mb, bsz, s, p['moba_q_g'], p['moba_k_g'], *rope)
    ot = _moba_attn(qT, k, vT, gate)
    yc = _ret_mixer(rt, bsz, s, ret_tables)
    return _out_proj(x2, ya, ot, yc, p['mix_out_g'], p['w_out'], bsz, s)


def _peer_layer(x2, p):
    h2, idx_t, gate_t = _peer_route(x2, p['norm2_g'], p['peer_wqT'], p['peer_sk'])
    idx = idx_t.T * TABLE_ROWS_PER_EXPERT
    gate_i = jnp.repeat(gate_t.T, 2, axis=1)[:, None, :]
    c_i = _peer_act(idx, h2, gate_i, p['peer_u'])
    return x2 + _peer_out(idx, c_i, p['peer_v']).reshape(x2.shape)


def kernel(x, norm1_g, w_in, s5_lam_re, s5_lam_im, s5_log_dt, s5_b_re, s5_b_im, s5_c_re, s5_c_im,
           s5_d, s5_w_glu, moba_q_g, moba_k_g, mix_out_g, w_out, norm2_g, peer_w_q, peer_sub_keys,
           peer_u, peer_v):
    bsz, s, d = x.shape
    depth = w_in.shape[0]
    pos = jnp.arange(s, dtype=F32)
    inv = ROPE_THETA ** (-jnp.arange(0, ROPE_DIMS, 2, dtype=F32) / ROPE_DIMS)
    ang = (pos[:, None] * inv[None, :]).T
    rope = (jnp.cos(ang), jnp.sin(ang))
    ret_tables = _ret_tables(s)
    x2 = x.reshape(bsz * s, d)
    for l in range(depth):
        bcat, ccat, tabs = _s5_operands(s5_lam_re[l], s5_lam_im[l], s5_log_dt[l], s5_b_re[l],
                                        s5_b_im[l], s5_c_re[l], s5_c_im[l])
        w_in_hi = w_in[l].astype(BF16)
        qk_cols = slice(S5_CH, S5_CH + 2 * MOBA_W)
        p = {
            'norm1_g': norm1_g[l][None, :],
            'w_in': w_in_hi,
            'w_in_qk_lo': (w_in[l][:, qk_cols] - w_in_hi[:, qk_cols].astype(F32)).astype(BF16),
            's5_bcat': bcat, 's5_ccat': ccat, 's5_tabs': tabs,
            's5_d': s5_d[l][None, :],
            's5_w_glu': s5_w_glu[l].astype(BF16),
            'moba_q_g': moba_q_g[l][:, None],
            'moba_k_g': moba_k_g[l][:, None],
            'mix_out_g': mix_out_g[l][None, :],
            'w_out': w_out[l].astype(BF16),
            'norm2_g': norm2_g[l][None, :],
            'peer_wqT': peer_w_q[l].T.astype(BF16),
            'peer_sk': peer_sub_keys[l].reshape(2 * PEER_HEADS, PEER_KEYS, PEER_QDIM // 2).astype(BF16),
            'peer_u': _pack_table(peer_u[l]),
            'peer_v': _pack_table(peer_v[l]),
        }
        x2 = _mixer_layer(x2, bsz, s, p, rope, ret_tables)
        x2 = _peer_layer(x2, p)
    return x2.reshape(bsz, s, d)
```

```python
import functools
import math

import numpy as np
import jax
import jax.numpy as jnp
from jax import lax
from jax.experimental import pallas as pl
from jax.experimental.pallas import tpu as pltpu

D_MODEL = 1024
HEAD_DIM = 64
S5_CH = 256
S5_GROUP = 16
S5_NGROUPS = 16
S5_STATE = 64
S5_W = S5_NGROUPS * S5_STATE
MOBA_HEADS = 8
MOBA_W = 512
MOBA_BLOCK = 256
MOBA_TOPK = 3
ROPE_THETA = 500000.0
ROPE_DIMS = 16
RET_HEADS = 4
RET_W = 256
RET_ANGLE_BASE = 10000.0
MIX_W = S5_CH + MOBA_W + RET_W
IN_W = S5_CH + 3 * MOBA_W + 4 * RET_W
PEER_KEYS = 128
PEER_HEADS = 8
PEER_TOPK = 16
PEER_QDIM = 128
PEER_SEL = PEER_HEADS * PEER_TOPK
EPS = 1e-6

LANES = 128
TABLE_ROWS_PER_EXPERT = D_MODEL // (2 * LANES)
TILE_STRIDE = 136
VMEM_LIMIT = 56 * 1024 * 1024

S5_CHUNK = 128
RET_CHUNK = 256
PROJ_ROWS = 256
PEER_ROUTE_TOKENS = 256
PEER_GATHER_TOKENS = 128

BF16 = jnp.bfloat16
F32 = jnp.float32
NEG_INF = float("-inf")


def _params(sem):
    return pltpu.CompilerParams(dimension_semantics=sem, vmem_limit_bytes=VMEM_LIMIT)


def _gelu_tanh(y):
    return 0.5 * y * (1.0 + jnp.tanh(0.7978845608028654 * (y + 0.044715 * (y * y * y))))


def _rms_scale(x, axis):
    return lax.rsqrt(jnp.mean(x * x, axis=axis, keepdims=True) + EPS)


def _split(a):
    hi = a.astype(BF16)
    return hi, (a - hi.astype(F32)).astype(BF16)


def _dot3(a, b):
    a_hi, a_lo = _split(a)
    b_hi, b_lo = _split(b)
    out = jnp.dot(a_hi, b_hi, preferred_element_type=F32)
    out = out + jnp.dot(a_hi, b_lo, preferred_element_type=F32)
    return out + jnp.dot(a_lo, b_hi, preferred_element_type=F32)


def _in_proj_kernel(x_ref, g_ref, w_ref, wlo_ref, ua_ref, mb_ref, rt_ref):
    x = x_ref[...]
    hf = x * _rms_scale(x, -1) * g_ref[...]
    h = hf.astype(BF16)
    ua_ref[...] = jnp.dot(h, w_ref[:, 0:S5_CH], preferred_element_type=F32)
    h_lo = (hf - h.astype(F32)).astype(BF16)
    q0, q1 = S5_CH, S5_CH + 2 * MOBA_W
    qk = jnp.dot(h, w_ref[:, q0:q1], preferred_element_type=F32)
    qk = qk + jnp.dot(h, wlo_ref[...], preferred_element_type=F32)
    qk = qk + jnp.dot(h_lo, w_ref[:, q0:q1], preferred_element_type=F32)
    mb_ref[:, 0:2 * MOBA_W] = qk
    mb_ref[:, 2 * MOBA_W:3 * MOBA_W] = jnp.dot(h, w_ref[:, q1:q1 + MOBA_W], preferred_element_type=F32)
    rt_ref[...] = jnp.dot(h, w_ref[:, S5_CH + 3 * MOBA_W:IN_W], preferred_element_type=F32)


def _in_proj(x2, g, w, wlo):
    t = x2.shape[0]
    tm = PROJ_ROWS
    return pl.pallas_call(
        _in_proj_kernel,
        grid=(t // tm,),
        in_specs=[pl.BlockSpec((tm, D_MODEL), lambda i: (i, 0)),
                  pl.BlockSpec((1, D_MODEL), lambda i: (0, 0)),
                  pl.BlockSpec((D_MODEL, IN_W), lambda i: (0, 0)),
                  pl.BlockSpec((D_MODEL, 2 * MOBA_W), lambda i: (0, 0))],
        out_specs=[pl.BlockSpec((tm, S5_CH), lambda i: (i, 0)),
                   pl.BlockSpec((tm, 3 * MOBA_W), lambda i: (i, 0)),
                   pl.BlockSpec((tm, 4 * RET_W), lambda i: (i, 0))],
        out_shape=[jax.ShapeDtypeStruct((t, S5_CH), F32),
                   jax.ShapeDtypeStruct((t, 3 * MOBA_W), F32),
                   jax.ShapeDtypeStruct((t, 4 * RET_W), F32)],
        compiler_params=_params(("parallel",)),
        name="in_proj",
    )(x2, g, w, wlo)


def _s5_kernel(u_ref, bcat_ref, ccat_ref, tab_ref, d_ref, wglu_ref, o_ref, st_ref):
    L = S5_CHUNK

    @pl.when(pl.program_id(1) == 0)
    def _():
        st_ref[...] = jnp.zeros_like(st_ref)

    u = u_ref[...]
    bu = jnp.dot(u.astype(BF16), bcat_ref[...], preferred_element_type=F32)
    bur, bui = bu[:, :S5_W], bu[:, S5_W:]
    air, aii = tab_ref[0], tab_ref[1]
    ktr = bur * air - bui * aii
    kti = bur * aii + bui * air
    row = lax.broadcasted_iota(jnp.int32, (L, L), 0)
    col = lax.broadcasted_iota(jnp.int32, (L, L), 1)
    tri = jnp.where(row >= col, 1.0, 0.0).astype(BF16)
    cr = jnp.dot(tri, ktr.astype(BF16), preferred_element_type=F32)
    ci = jnp.dot(tri, kti.astype(BF16), preferred_element_type=F32)
    apr, api = tab_ref[2], tab_ref[3]
    acr, aci = tab_ref[4], tab_ref[5]
    pr = st_ref[0:1, :]
    pi = st_ref[1:2, :]
    sr = apr * cr - api * ci + acr * pr - aci * pi
    si = apr * ci + api * cr + acr * pi + aci * pr
    st_ref[0:1, :] = sr[L - 1:L, :]
    st_ref[1:2, :] = si[L - 1:L, :]
    y = (jnp.dot(sr.astype(BF16), ccat_ref[0:S5_W, :], preferred_element_type=F32)
         + jnp.dot(si.astype(BF16), ccat_ref[S5_W:2 * S5_W, :], preferred_element_type=F32))
    y = _gelu_tanh(y + u * d_ref[...])
    z = jnp.dot(y.astype(BF16), wglu_ref[...], preferred_element_type=F32)
    y = y * jax.nn.sigmoid(z)
    o_ref[...] = y * _rms_scale(y, -1)


def _s5_operands(lam_re, lam_im, log_dt, b_re, b_im, c_re, c_im):
    dt = jnp.exp(log_dt)[:, None]
    mag = jnp.exp(lam_re * dt)
    ar = mag * jnp.cos(lam_im * dt)
    ai = mag * jnp.sin(lam_im * dt)
    den = lam_re * lam_re + lam_im * lam_im
    fr = ((ar - 1.0) * lam_re + ai * lam_im) / den
    fi = (ai * lam_re - (ar - 1.0) * lam_im) / den
    bbr = fr[..., None] * b_re - fi[..., None] * b_im
    bbi = fr[..., None] * b_im + fi[..., None] * b_re
    eye = jnp.eye(S5_NGROUPS, dtype=F32)
    blk = lambda m: jnp.einsum('gph,gk->ghkp', m, eye).reshape(S5_CH, S5_W)
    bcat = jnp.concatenate([blk(bbr), blk(bbi)], axis=1).astype(BF16)
    blc = lambda m: jnp.einsum('ghp,gk->gpkh', m, eye).reshape(S5_W, S5_CH)
    ccat = jnp.concatenate([blc(c_re), -blc(c_im)], axis=0).astype(BF16)

    def powers(n):
        lr = (lam_re * dt).reshape(1, S5_W)
        li = (lam_im * dt).reshape(1, S5_W)
        m = jnp.exp(lr * n[:, None])
        return m * jnp.cos(li * n[:, None]), m * jnp.sin(li * n[:, None])

    j = jnp.arange(S5_CHUNK, dtype=F32)
    tabs = jnp.stack([*powers(-j), *powers(j), *powers(j + 1.0)], axis=0)
    return bcat, ccat, tabs


def _s5_mixer(ua, bsz, s, bcat, ccat, tabs, d_skip, w_glu):
    L = S5_CHUNK
    nc = s // L
    return pl.pallas_call(
        _s5_kernel,
        grid=(bsz, nc),
        in_specs=[pl.BlockSpec((L, S5_CH), lambda b, j: (b * nc + j, 0)),
                  pl.BlockSpec((S5_CH, 2 * S5_W), lambda b, j: (0, 0)),
                  pl.BlockSpec((2 * S5_W, S5_CH), lambda b, j: (0, 0)),
                  pl.BlockSpec((6, L, S5_W), lambda b, j: (0, 0, 0)),
                  pl.BlockSpec((1, S5_CH), lambda b, j: (0, 0)),
                  pl.BlockSpec((S5_CH, S5_CH), lambda b, j: (0, 0))],
        out_specs=pl.BlockSpec((L, S5_CH), lambda b, j: (b * nc + j, 0)),
        out_shape=jax.ShapeDtypeStruct((bsz * s, S5_CH), F32),
        scratch_shapes=[pltpu.VMEM((2, S5_W), F32)],
        compiler_params=_params(("parallel", "arbitrary")),
        name="s5_mixer",
    )(ua, bcat, ccat, tabs, d_skip, w_glu)


def _moba_prep_kernel(qkv_ref, gq_ref, gk_ref, cos_ref, sin_ref, qT_ref, k_ref, vT_ref, gate_ref, km_ref):
    j = pl.program_id(1)

    @pl.when(j == 0)
    def _():
        km_ref[...] = jnp.zeros_like(km_ref)

    x = qkv_ref[...]
    c = cos_ref[...]
    s = sin_ref[...]
    half = ROPE_DIMS // 2

    def prep(xT, g):
        y = xT * _rms_scale(xT, 0) * g
        x1 = y[0:half]
        x2 = y[half:ROPE_DIMS]
        return jnp.concatenate([x1 * c - x2 * s, x2 * c + x1 * s, y[ROPE_DIMS:]], axis=0)

    qT_all = x[:, 0:MOBA_W].T
    kT_all = x[:, MOBA_W:2 * MOBA_W].T
    vT_all = x[:, 2 * MOBA_W:3 * MOBA_W].T
    gq = gq_ref[...]
    gk = gk_ref[...]
    for h in range(MOBA_HEADS):
        sl = slice(h * HEAD_DIM, (h + 1) * HEAD_DIM)
        q = prep(qT_all[sl], gq) * (HEAD_DIM ** -0.5)
        qT_ref[h] = q.astype(BF16)
        gate_ref[h] = _dot3(km_ref[h], q)
        k = prep(kT_all[sl], gk).T
        k_ref[h] = k.astype(BF16)
        km_ref[h, pl.ds(j, 1), :] = jnp.mean(k, axis=0, keepdims=True)
        vT_ref[h] = vT_all[sl].astype(BF16)


def _moba_prep(qkv, bsz, s, gq, gk, cosT, sinT):
    nb = s // MOBA_BLOCK
    H, dh, blk = MOBA_HEADS, HEAD_DIM, MOBA_BLOCK
    return pl.pallas_call(
        _moba_prep_kernel,
        grid=(bsz, nb),
        in_specs=[pl.BlockSpec((blk, 3 * MOBA_W), lambda b, j: (b * nb + j, 0)),
                  pl.BlockSpec((dh, 1), lambda b, j: (0, 0)),
                  pl.BlockSpec((dh, 1), lambda b, j: (0, 0)),
                  pl.BlockSpec((ROPE_DIMS // 2, blk), lambda b, j: (0, j)),
                  pl.BlockSpec((ROPE_DIMS // 2, blk), lambda b, j: (0, j))],
        out_specs=[pl.BlockSpec((None, H, dh, blk), lambda b, j: (b, 0, 0, j)),
                   pl.BlockSpec((None, H, None, blk, dh), lambda b, j: (b, 0, j, 0, 0)),
                   pl.BlockSpec((None, H, None, dh, blk), lambda b, j: (b, 0, j, 0, 0)),
                   pl.BlockSpec((None, H, nb, blk), lambda b, j: (b, 0, 0, j))],
        out_shape=[jax.ShapeDtypeStruct((bsz, H, dh, s), BF16),
                   jax.ShapeDtypeStruct((bsz, H, nb, blk, dh), BF16),
                   jax.ShapeDtypeStruct((bsz, H, nb, dh, blk), BF16),
                   jax.ShapeDtypeStruct((bsz, H, nb, s), F32)],
        scratch_shapes=[pltpu.VMEM((H, nb, dh), F32)],
        compiler_params=_params(("parallel", "arbitrary")),
        name="moba_prep",
    )(qkv, gq, gk, cosT, sinT)


def _moba_attn_kernel(qT_ref, k_ref, vT_ref, gate_ref, o_ref, msk_ref):
    i = pl.program_id(2)
    nb = gate_ref.shape[0]
    blk = MOBA_BLOCK
    qT = qT_ref[...]
    gate = gate_ref[...]
    row = lax.broadcasted_iota(jnp.int32, (nb, blk), 0)
    gm = jnp.where(row < i, gate, NEG_INF)
    cnt = jnp.zeros((nb, blk), jnp.int32)
    for m in range(nb):
        gmm = gm[m:m + 1, :]
        beats = jnp.where(gmm > gm, 1, jnp.where(gmm == gm, jnp.where(row > m, 1, 0), 0))
        cnt = cnt + beats
    sel = jnp.where(row < i, jnp.where(cnt < MOBA_TOPK, 0.0, NEG_INF), NEG_INF)
    msk_ref[...] = sel

    kpos = lax.broadcasted_iota(jnp.int32, (blk, blk), 0)
    qpos = lax.broadcasted_iota(jnp.int32, (blk, blk), 1)
    sT = jnp.dot(k_ref[i], qT, preferred_element_type=F32)
    sT = jnp.where(kpos <= qpos, sT, NEG_INF)
    m0 = jnp.max(sT, axis=0, keepdims=True)
    p = jnp.exp(sT - m0)
    l0 = jnp.sum(p, axis=0, keepdims=True)
    acc0 = jnp.dot(vT_ref[i], p.astype(BF16), preferred_element_type=F32)

    def body(n, carry):
        m, l, acc = carry
        sT = jnp.dot(k_ref[n], qT, preferred_element_type=F32) + msk_ref[pl.ds(n, 1), :]
        m_new = jnp.maximum(m, jnp.max(sT, axis=0, keepdims=True))
        alpha = jnp.exp(m - m_new)
        p = jnp.exp(sT - m_new)
        l = alpha * l + jnp.sum(p, axis=0, keepdims=True)
        acc = alpha * acc + jnp.dot(vT_ref[n], p.astype(BF16), preferred_element_type=F32)
        return m_new, l, acc

    _, l, acc = lax.fori_loop(0, i, body, (m0, l0, acc0))
    o_ref[...] = acc / l


def _moba_attn(qT, k, vT, gate):
    bsz, H, dh, s = qT.shape
    nb = s // MOBA_BLOCK
    blk = MOBA_BLOCK
    return pl.pallas_call(
        _moba_attn_kernel,
        grid=(bsz, H, nb),
        in_specs=[pl.BlockSpec((None, None, dh, blk), lambda b, h, i: (b, h, 0, i)),
                  pl.BlockSpec((None, None, nb, blk, dh), lambda b, h, i: (b, h, 0, 0, 0)),
                  pl.BlockSpec((None, None, nb, dh, blk), lambda b, h, i: (b, h, 0, 0, 0)),
                  pl.BlockSpec((None, None, nb, blk), lambda b, h, i: (b, h, 0, i))],
        out_specs=pl.BlockSpec((None, None, dh, blk), lambda b, h, i: (b, h, 0, i)),
        out_shape=jax.ShapeDtypeStruct((bsz, H, dh, s), F32),
        scratch_shapes=[pltpu.VMEM((nb, blk), F32)],
        compiler_params=_params(("parallel", "parallel", "arbitrary")),
        name="moba_attn",
    )(qT, k, vT, gate)


def _ret_log_decay():
    return np.log(1.0 - 2.0 ** (-5.0 - np.arange(RET_HEADS, dtype=np.float64)))


def _ret_kernel(x_ref, cos_ref, sin_ref, xi_ref, zeta_ref, dm_ref, o_ref, st_ref):
    C = RET_CHUNK

    @pl.when(pl.program_id(1) == 0)
    def _():
        st_ref[...] = jnp.zeros_like(st_ref)

    x = x_ref[...]
    q = x[:, 0:RET_W]
    k = x[:, RET_W:2 * RET_W]
    v = x[:, 2 * RET_W:3 * RET_W]
    g = x[:, 3 * RET_W:4 * RET_W]
    cos = cos_ref[...]
    sin = sin_ref[...]
    lane = lax.broadcasted_iota(jnp.int32, (C, RET_W), 1)
    first = (lane & (HEAD_DIM // 2)) == 0

    def rot(t):
        swapped = jnp.where(first, pltpu.roll(t, RET_W - HEAD_DIM // 2, 1), pltpu.roll(t, HEAD_DIM // 2, 1))
        return t * cos + swapped * sin

    qr = rot(q)
    kr = rot(k) * (HEAD_DIM ** -0.5)
    qb = qr.astype(BF16)
    kb = kr.astype(BF16)
    vb = v.astype(BF16)
    qx = (qr * xi_ref[...]).astype(BF16)
    kz = kr * zeta_ref[...]
    decay_c = np.exp(C * _ret_log_decay())
    for h in range(RET_HEADS):
        sl = slice(h * HEAD_DIM, (h + 1) * HEAD_DIM)
        sc = lax.dot_general(qb[:, sl], kb[:, sl], (((1,), (1,)), ((), ())),
                             preferred_element_type=F32) * dm_ref[h]
        inner = jnp.dot(sc.astype(BF16), vb[:, sl], preferred_element_type=F32)
        r_prev = st_ref[h]
        cross = jnp.dot(qx[:, sl], r_prev.astype(BF16), preferred_element_type=F32)
        kv = jnp.dot(kz[:, sl].T.astype(BF16), vb[:, sl], preferred_element_type=F32)
        st_ref[h] = float(decay_c[h]) * r_prev + kv
        o = inner + cross
        o = o * _rms_scale(o, -1)
        gh = g[:, sl]
        o_ref[:, sl] = gh * jax.nn.sigmoid(gh) * o


def _ret_tables(s):
    C = RET_CHUNK
    half = HEAD_DIM // 2
    pos = jnp.arange(s, dtype=F32)
    inv = RET_ANGLE_BASE ** (-jnp.linspace(0.0, 1.0, half, dtype=F32))
    ang = pos[:, None] * inv[None, :]
    cos, sin = jnp.cos(ang), jnp.sin(ang)
    cos_t = jnp.tile(cos, (1, 2 * RET_HEADS))
    sin_t = jnp.tile(jnp.concatenate([-sin, sin], axis=1), (1, RET_HEADS))
    lg = _ret_log_decay()
    i = np.arange(C, dtype=np.float64)
    xi = np.repeat(np.exp((i + 1.0)[:, None] * lg[None, :]), HEAD_DIM, axis=1)
    zeta = np.repeat(np.exp((C - 1.0 - i)[:, None] * lg[None, :]), HEAD_DIM, axis=1)
    diff = i[:, None] - i[None, :]
    dm = np.where(diff >= 0, np.exp(np.maximum(diff, 0.0)[None] * lg[:, None, None]), 0.0)
    return cos_t, sin_t, jnp.asarray(xi, F32), jnp.asarray(zeta, F32), jnp.asarray(dm, F32)


def _ret_mixer(rt, bsz, s, tables):
    C = RET_CHUNK
    nc = s // C
    cos_t, sin_t, xi, zeta, dm = tables
    return pl.pallas_call(
        _ret_kernel,
        grid=(bsz, nc),
        in_specs=[pl.BlockSpec((C, 4 * RET_W), lambda b, j: (b * nc + j, 0)),
                  pl.BlockSpec((C, RET_W), lambda b, j: (j, 0)),
                  pl.BlockSpec((C, RET_W), lambda b, j: (j, 0)),
                  pl.BlockSpec((C, RET_W), lambda b, j: (0, 0)),
                  pl.BlockSpec((C, RET_W), lambda b, j: (0, 0)),
                  pl.BlockSpec((RET_HEADS, C, C), lambda b, j: (0, 0, 0))],
        out_specs=pl.BlockSpec((C, RET_W), lambda b, j: (b * nc + j, 0)),
        out_shape=jax.ShapeDtypeStruct((bsz * s, RET_W), F32),
        scratch_shapes=[pltpu.VMEM((RET_HEADS, HEAD_DIM, HEAD_DIM), F32)],
        compiler_params=_params(("parallel", "arbitrary")),
        name="ret_mixer",
    )(rt, cos_t, sin_t, xi, zeta, dm)


def _out_proj_kernel(x_ref, ya_ref, ot_ref, yc_ref, g_ref, w_ref, o_ref):
    ts = x_ref.shape[0]
    ot = ot_ref[...]
    yb = (ot * _rms_scale(ot, 1)).reshape(MOBA_W, ts).T
    g = g_ref[...]
    a0, a1 = S5_CH, S5_CH + MOBA_W
    acc = x_ref[...]
    acc = acc + jnp.dot((ya_ref[...] * g[:, 0:a0]).astype(BF16), w_ref[0:a0, :], preferred_element_type=F32)
    acc = acc + jnp.dot((yb * g[:, a0:a1]).astype(BF16), w_ref[a0:a1, :], preferred_element_type=F32)
    acc = acc + jnp.dot((yc_ref[...] * g[:, a1:MIX_W]).astype(BF16), w_ref[a1:MIX_W, :],
                        preferred_element_type=F32)
    o_ref[...] = acc


def _out_proj(x2, ya, ot, yc, g, w, bsz, s):
    ts = PROJ_ROWS
    nt = s // ts
    return pl.pallas_call(
        _out_proj_kernel,
        grid=(bsz, nt),
        in_specs=[pl.BlockSpec((ts, D_MODEL), lambda b, j: (b * nt + j, 0)),
                  pl.BlockSpec((ts, S5_CH), lambda b, j: (b * nt + j, 0)),
                  pl.BlockSpec((None, MOBA_HEADS, HEAD_DIM, ts), lambda b, j: (b, 0, 0, j)),
                  pl.BlockSpec((ts, RET_W), lambda b, j: (b * nt + j, 0)),
                  pl.BlockSpec((1, MIX_W), lambda b, j: (0, 0)),
                  pl.BlockSpec((MIX_W, D_MODEL), lambda b, j: (0, 0))],
        out_specs=pl.BlockSpec((ts, D_MODEL), lambda b, j: (b * nt + j, 0)),
        out_shape=jax.ShapeDtypeStruct((bsz * s, D_MODEL), F32),
        compiler_params=_params(("parallel", "parallel")),
        name="out_proj",
    )(x2, ya, ot, yc, g, w)


def _top16(s, payload):
    n = s.shape[0]
    io = lax.broadcasted_iota(jnp.int32, s.shape, 0)
    vals, pays = [], []
    for _ in range(PEER_TOPK):
        m = jnp.max(s, axis=0, keepdims=True)
        pos = jnp.min(jnp.where(s == m, io, n), axis=0, keepdims=True)
        hit = io == pos
        vals.append(m)
        pays.append(jnp.sum(jnp.where(hit, payload, 0), axis=0, keepdims=True))
        s = jnp.where(hit, NEG_INF, s)
    return jnp.concatenate(vals, axis=0), jnp.concatenate(pays, axis=0)


def _peer_route_kernel(x_ref, g_ref, wqT_ref, sk_ref, h2_ref, idx_ref, gate_ref, q_scr):
    tb = x_ref.shape[0]
    x = x_ref[...]
    h2 = x * _rms_scale(x, -1) * g_ref[...]
    h2_ref[...] = h2
    q_scr[...] = lax.dot_general(wqT_ref[...], h2.astype(BF16), (((1,), (1,)), ((), ())),
                                 preferred_element_type=F32).astype(BF16)
    half = PEER_QDIM // 2
    io_keys = lax.broadcasted_iota(jnp.int32, (PEER_KEYS, tb), 0)

    def head(h, carry):
        sub = []
        for c in range(2):
            grp = 2 * h + c
            qg = q_scr[pl.ds(pl.multiple_of(grp * half, half), half), :]
            sc = jnp.dot(sk_ref[grp], qg, preferred_element_type=F32)
            sub.append(_top16(sc, io_keys))
        (v0, i0), (v1, i1) = sub
        cand = jnp.concatenate([v0[a:a + 1, :] + v1 for a in range(PEER_TOPK)], axis=0)
        cid = jnp.concatenate([i0[a:a + 1, :] * PEER_KEYS + i1 for a in range(PEER_TOPK)], axis=0)
        fs, eidx = _top16(cand, cid)
        e = jnp.exp(fs - fs[0:1, :])
        gates = e / jnp.sum(e, axis=0, keepdims=True)
        r0 = pl.multiple_of(h * PEER_TOPK, PEER_TOPK)
        idx_ref[pl.ds(r0, PEER_TOPK), :] = eidx
        gate_ref[pl.ds(r0, PEER_TOPK), :] = gates
        return carry

    lax.fori_loop(0, PEER_HEADS, head, 0)


def _peer_route(x2, g, wqT, sk):
    t = x2.shape[0]
    tb = PEER_ROUTE_TOKENS
    ngrp = 2 * PEER_HEADS
    return pl.pallas_call(
        _peer_route_kernel,
        grid=(t // tb,),
        in_specs=[pl.BlockSpec((tb, D_MODEL), lambda i: (i, 0)),
                  pl.BlockSpec((1, D_MODEL), lambda i: (0, 0)),
                  pl.BlockSpec((PEER_HEADS * PEER_QDIM, D_MODEL), lambda i: (0, 0)),
                  pl.BlockSpec((ngrp, PEER_KEYS, PEER_QDIM // 2), lambda i: (0, 0, 0))],
        out_specs=[pl.BlockSpec((tb, D_MODEL), lambda i: (i, 0)),
                   pl.BlockSpec((PEER_SEL, tb), lambda i: (0, i)),
                   pl.BlockSpec((PEER_SEL, tb), lambda i: (0, i))],
        out_shape=[jax.ShapeDtypeStruct((t, D_MODEL), F32),
                   jax.ShapeDtypeStruct((PEER_SEL, t), jnp.int32),
                   jax.ShapeDtypeStruct((PEER_SEL, t), F32)],
        scratch_shapes=[pltpu.VMEM((PEER_HEADS * PEER_QDIM, tb), BF16)],
        compiler_params=_params(("parallel",)),
        name="peer_route",
    )(x2, g, wqT, sk)


def _pack_table(tab):
    bits = lax.bitcast_convert_type(tab.astype(BF16), jnp.uint16).astype(jnp.uint32)
    half = D_MODEL // 2
    words = bits[:, :half] | (bits[:, half:] << 16)
    return lax.bitcast_convert_type(words, jnp.int32).reshape(tab.shape[0] * TABLE_ROWS_PER_EXPERT, LANES)


def _table_spec(tab):
    return pl.BlockSpec(tab.shape, lambda i: (0, 0), pipeline_mode=pl.Buffered(1))


def _gather_rows(idx_ref, t, tab_ref, tile_ref):
    R = TABLE_ROWS_PER_EXPERT
    for k in range(PEER_SEL):
        row = pl.multiple_of(idx_ref[t, k], R)
        tile_ref[pl.ds(k, R, stride=TILE_STRIDE), :] = tab_ref[pl.ds(row, R), :]


def _tile_chunk(tile_ref, r):
    return pltpu.bitcast(tile_ref[r * TILE_STRIDE:r * TILE_STRIDE + PEER_SEL, :], BF16)


def _pipelined_tokens(tb, idx_ref, tab_ref, tiles, compute, finish, init):
    tile_a, tile_b = tiles
    _gather_rows(idx_ref, 0, tab_ref, tile_a)
    _gather_rows(idx_ref, 1, tab_ref, tile_b)

    def pair(i, pending):
        t0 = 2 * i
        finish(jnp.maximum(t0 - 1, 0), pending)
        va = compute(t0, tile_a)
        _gather_rows(idx_ref, jnp.minimum(t0 + 2, tb - 1), tab_ref, tile_a)
        finish(t0, va)
        vb = compute(t0 + 1, tile_b)
        _gather_rows(idx_ref, jnp.minimum(t0 + 3, tb - 1), tab_ref, tile_b)
        return vb

    last = lax.fori_loop(0, tb // 2, pair, init)
    finish(tb - 1, last)


def _hi_lo_rows(v, lane_parity):
    n = v.shape[1]
    vb = jnp.broadcast_to(v, (8, n))
    hi = vb.astype(BF16).astype(F32)
    srow = lax.broadcasted_iota(jnp.int32, (8, n), 0)
    part = jnp.where(srow < 2, hi, vb - hi)
    part = jnp.where(srow < 4, part, 0.0)
    return jnp.where((srow & 1) == lane_parity, part, 0.0)


def _peer_act_kernel(idx_ref, x_ref, gate_ref, tab_ref, c_ref, tile_a, tile_b):
    tb = x_ref.shape[0]
    R = TABLE_ROWS_PER_EXPERT
    half = D_MODEL // 2
    lane = lax.broadcasted_iota(jnp.int32, (PEER_SEL, tb), 1)
    c_ref[...] = jnp.zeros_like(c_ref)

    def compute(t, tile_ref):
        x = x_ref[pl.ds(t, 1), :]
        acc = jnp.zeros((PEER_SEL, LANES), F32)
        for r in range(R):
            w = tile_ref[r * TILE_STRIDE:r * TILE_STRIDE + PEER_SEL, :]
            lo = lax.bitcast_convert_type(w << 16, F32)
            hi = lax.bitcast_convert_type(w & jnp.int32(-65536), F32)
            acc = acc + lo * x[:, r * LANES:(r + 1) * LANES]
            acc = acc + hi * x[:, half + r * LANES:half + (r + 1) * LANES]
        return jnp.sum(acc, axis=1, keepdims=True)

    def finish(t, col):
        c_ref[...] = jnp.where(lane == t, col, c_ref[...])

    _pipelined_tokens(tb, idx_ref, tab_ref, (tile_a, tile_b), compute, finish,
                      jnp.zeros((PEER_SEL, 1), F32))
    c_ref[...] = gate_ref[...] * _gelu_tanh(c_ref[...])


def _tile_scratch():
    shape = (TABLE_ROWS_PER_EXPERT * TILE_STRIDE, LANES)
    return [pltpu.VMEM(shape, jnp.int32), pltpu.VMEM(shape, jnp.int32)]


def _peer_act(idx, h2, gate_t, tab):
    t = h2.shape[0]
    tb = PEER_GATHER_TOKENS
    return pl.pallas_call(
        _peer_act_kernel,
        grid=(t // tb,),
        in_specs=[pl.BlockSpec((tb, PEER_SEL), lambda i: (i, 0), memory_space=pltpu.SMEM),
                  pl.BlockSpec((tb, D_MODEL), lambda i: (i, 0)),
                  pl.BlockSpec((PEER_SEL, tb), lambda i: (0, i)),
                  _table_spec(tab)],
        out_specs=pl.BlockSpec((PEER_SEL, tb), lambda i: (0, i)),
        out_shape=jax.ShapeDtypeStruct((PEER_SEL, t), F32),
        scratch_shapes=_tile_scratch(),
        compiler_params=_params(("arbitrary",)),
        name="peer_act",
    )(idx, h2, gate_t, tab)


def _peer_out_kernel(idx_ref, c_ref, tab_ref, o_ref, tile_a, tile_b):
    tb = o_ref.shape[0]
    R = TABLE_ROWS_PER_EXPERT
    half = D_MODEL // 2
    lane_parity = lax.broadcasted_iota(jnp.int32, (8, 2 * PEER_SEL), 1) & 1

    def compute(t, tile_ref):
        rows = _hi_lo_rows(c_ref[t], lane_parity).astype(BF16)
        res = [jnp.dot(rows, _tile_chunk(tile_ref, r), preferred_element_type=F32) for r in range(R)]
        first = jnp.concatenate([v[0:1] + v[2:3] for v in res], axis=1)
        second = jnp.concatenate([v[1:2] + v[3:4] for v in res], axis=1)
        return jnp.concatenate([first, second], axis=1)

    def finish(t, row):
        o_ref[t] = row

    _pipelined_tokens(tb, idx_ref, tab_ref, (tile_a, tile_b), compute, finish,
                      jnp.zeros((1, D_MODEL), F32))


def _peer_out(idx, c_i, tab):
    t = idx.shape[0]
    tb = PEER_GATHER_TOKENS
    return pl.pallas_call(
        _peer_out_kernel,
        grid=(t // tb,),
        in_specs=[pl.BlockSpec((tb, PEER_SEL), lambda i: (i, 0), memory_space=pltpu.SMEM),
                  pl.BlockSpec((tb, 1, 2 * PEER_SEL), lambda i: (i, 0, 0)),
                  _table_spec(tab)],
        out_specs=pl.BlockSpec((tb, 1, D_MODEL), lambda i: (i, 0, 0)),
        out_shape=jax.ShapeDtypeStruct((t, 1, D_MODEL), F32),
        scratch_shapes=_tile_scratch(),
        compiler_params=_params(("arbitrary",)),
        name="peer_out",
    )(idx, c_i, tab)


def _mixer_layer(x2, bsz, s, p, rope, ret_tables):
    ua, mb, rt = _in_proj(x2, p['norm1_g'], p['w_in'], p['w_in_qk_lo'])
    ya = _s5_mixer(ua, bsz, s, p['s5_bcat'], p['s5_ccat'], p['s5_tabs'], p['s5_d'], p['s5_w_glu'])
    qT, k, vT, gate = _moba_prep(mb, bsz, s, p['moba_q_g'], p['moba_k_g'], *rope)
    ot = _moba_attn(qT, k, vT, gate)
    yc = _ret_mixer(rt, bsz, s, ret_tables)
    return _out_proj(x2, ya, ot, yc, p['mix_out_g'], p['w_out'], bsz, s)


def _peer_layer(x2, p):
    h2, idx_t, gate_t = _peer_route(x2, p['norm2_g'], p['peer_wqT'], p['peer_sk'])
    idx = idx_t.T * TABLE_ROWS_PER_EXPERT
    c_t = _peer_act(idx, h2, gate_t, p['peer_u'])
    c_i = jnp.repeat(c_t.T, 2, axis=1)[:, None, :]
    return x2 + _peer_out(idx, c_i, p['peer_v']).reshape(x2.shape)


def kernel(x, norm1_g, w_in, s5_lam_re, s5_lam_im, s5_log_dt, s5_b_re, s5_b_im, s5_c_re, s5_c_im,
           s5_d, s5_w_glu, moba_q_g, moba_k_g, mix_out_g, w_out, norm2_g, peer_w_q, peer_sub_keys,
           peer_u, peer_v):
    bsz, s, d = x.shape
    depth = w_in.shape[0]
    pos = jnp.arange(s, dtype=F32)
    inv = ROPE_THETA ** (-jnp.arange(0, ROPE_DIMS, 2, dtype=F32) / ROPE_DIMS)
    ang = (pos[:, None] * inv[None, :]).T
    rope = (jnp.cos(ang), jnp.sin(ang))
    ret_tables = _ret_tables(s)
    x2 = x.reshape(bsz * s, d)
    for l in range(depth):
        bcat, ccat, tabs = _s5_operands(s5_lam_re[l], s5_lam_im[l], s5_log_dt[l], s5_b_re[l],
                                        s5_b_im[l], s5_c_re[l], s5_c_im[l])
        w_in_hi = w_in[l].astype(BF16)
        qk_cols = slice(S5_CH, S5_CH + 2 * MOBA_W)
        p = {
            'norm1_g': norm1_g[l][None, :],
            'w_in': w_in_hi,
            'w_in_qk_lo': (w_in[l][:, qk_cols] - w_in_hi[:, qk_cols].astype(F32)).astype(BF16),
            's5_bcat': bcat, 's5_ccat': ccat, 's5_tabs': tabs,
            's5_d': s5_d[l][None, :],
            's5_w_glu': s5_w_glu[l].astype(BF16),
            'moba_q_g': moba_q_g[l][:, None],
            'moba_k_g': moba_k_g[l][:, None],
            'mix_out_g': mix_out_g[l][None, :],
            'w_out': w_out[l].astype(BF16),
            'norm2_g': norm2_g[l][None, :],
            'peer_wqT': peer_w_q[l].T.astype(BF16),
            'peer_sk': peer_sub_keys[l].reshape(2 * PEER_HEADS, PEER_KEYS, PEER_QDIM // 2).astype(BF16),
            'peer_u': _pack_table(peer_u[l]),
            'peer_v': _pack_table(peer_v[l]),
        }
        x2 = _mixer_layer(x2, bsz, s, p, rope, ret_tables)
        x2 = _peer_layer(x2, p)
    return x2.reshape(bsz, s, d)
```

```python
import functools
import math

import numpy as np
import jax
import jax.numpy as jnp
from jax import lax
from jax.experimental import pallas as pl
from jax.experimental.pallas import tpu as pltpu

D_MODEL = 1024
HEAD_DIM = 64
S5_CH = 256
S5_GROUP = 16
S5_NGROUPS = 16
S5_STATE = 64
S5_W = S5_NGROUPS * S5_STATE
MOBA_HEADS = 8
MOBA_W = 512
MOBA_BLOCK = 256
MOBA_TOPK = 3
MOBA_GROUP = 4
ROPE_THETA = 500000.0
ROPE_DIMS = 16
RET_HEADS = 4
RET_W = 256
RET_ANGLE_BASE = 10000.0
MIX_W = S5_CH + MOBA_W + RET_W
IN_W = S5_CH + 3 * MOBA_W + 4 * RET_W
PEER_KEYS = 128
PEER_HEADS = 8
PEER_TOPK = 16
PEER_QDIM = 128
PEER_SEL = PEER_HEADS * PEER_TOPK
EPS = 1e-6

LANES = 128
TABLE_ROWS_PER_EXPERT = D_MODEL // (2 * LANES)
TILE_STRIDE = 136
VMEM_LIMIT = 56 * 1024 * 1024

S5_CHUNK = 128
RET_CHUNK = 256
PROJ_ROWS = 256
PEER_ROUTE_TOKENS = 256
PEER_GATHER_TOKENS = 128

BF16 = jnp.bfloat16
F32 = jnp.float32
NEG_INF = float("-inf")


def _params(sem):
    return pltpu.CompilerParams(dimension_semantics=sem, vmem_limit_bytes=VMEM_LIMIT)


def _gelu_tanh(y):
    return 0.5 * y * (1.0 + jnp.tanh(0.7978845608028654 * (y + 0.044715 * (y * y * y))))


def _rms_scale(x, axis):
    return lax.rsqrt(jnp.mean(x * x, axis=axis, keepdims=True) + EPS)


def _split(a):
    hi = a.astype(BF16)
    return hi, (a - hi.astype(F32)).astype(BF16)


def _dot3(a, b):
    a_hi, a_lo = _split(a)
    b_hi, b_lo = _split(b)
    out = jnp.dot(a_hi, b_hi, preferred_element_type=F32)
    out = out + jnp.dot(a_hi, b_lo, preferred_element_type=F32)
    return out + jnp.dot(a_lo, b_hi, preferred_element_type=F32)


def _in_proj_kernel(x_ref, g_ref, w_ref, wlo_ref, ua_ref, mb_ref, rt_ref):
    x = x_ref[...]
    hf = x * _rms_scale(x, -1) * g_ref[...]
    h = hf.astype(BF16)
    ua_ref[...] = jnp.dot(h, w_ref[:, 0:S5_CH], preferred_element_type=F32)
    h_lo = (hf - h.astype(F32)).astype(BF16)
    q0, q1 = S5_CH, S5_CH + 2 * MOBA_W
    qk = jnp.dot(h, w_ref[:, q0:q1], preferred_element_type=F32)
    qk = qk + jnp.dot(h, wlo_ref[...], preferred_element_type=F32)
    qk = qk + jnp.dot(h_lo, w_ref[:, q0:q1], preferred_element_type=F32)
    mb_ref[:, 0:2 * MOBA_W] = qk
    mb_ref[:, 2 * MOBA_W:3 * MOBA_W] = jnp.dot(h, w_ref[:, q1:q1 + MOBA_W], preferred_element_type=F32)
    rt_ref[...] = jnp.dot(h, w_ref[:, S5_CH + 3 * MOBA_W:IN_W], preferred_element_type=F32)


def _in_proj(x2, g, w, wlo):
    t = x2.shape[0]
    tm = PROJ_ROWS
    return pl.pallas_call(
        _in_proj_kernel,
        grid=(t // tm,),
        in_specs=[pl.BlockSpec((tm, D_MODEL), lambda i: (i, 0)),
                  pl.BlockSpec((1, D_MODEL), lambda i: (0, 0)),
                  pl.BlockSpec((D_MODEL, IN_W), lambda i: (0, 0)),
                  pl.BlockSpec((D_MODEL, 2 * MOBA_W), lambda i: (0, 0))],
        out_specs=[pl.BlockSpec((tm, S5_CH), lambda i: (i, 0)),
                   pl.BlockSpec((tm, 3 * MOBA_W), lambda i: (i, 0)),
                   pl.BlockSpec((tm, 4 * RET_W), lambda i: (i, 0))],
        out_shape=[jax.ShapeDtypeStruct((t, S5_CH), F32),
                   jax.ShapeDtypeStruct((t, 3 * MOBA_W), F32),
                   jax.ShapeDtypeStruct((t, 4 * RET_W), F32)],
        compiler_params=_params(("parallel",)),
        name="in_proj",
    )(x2, g, w, wlo)


def _s5_kernel(u_ref, bcat_ref, ccat_ref, tab_ref, d_ref, wglu_ref, o_ref, st_ref):
    L = S5_CHUNK

    @pl.when(pl.program_id(1) == 0)
    def _():
        st_ref[...] = jnp.zeros_like(st_ref)

    u = u_ref[...]
    bu = jnp.dot(u.astype(BF16), bcat_ref[...], preferred_element_type=F32)
    bur, bui = bu[:, :S5_W], bu[:, S5_W:]
    air, aii = tab_ref[0], tab_ref[1]
    ktr = bur * air - bui * aii
    kti = bur * aii + bui * air
    row = lax.broadcasted_iota(jnp.int32, (L, L), 0)
    col = lax.broadcasted_iota(jnp.int32, (L, L), 1)
    tri = jnp.where(row >= col, 1.0, 0.0).astype(BF16)
    cr = jnp.dot(tri, ktr.astype(BF16), preferred_element_type=F32)
    ci = jnp.dot(tri, kti.astype(BF16), preferred_element_type=F32)
    apr, api = tab_ref[2], tab_ref[3]
    acr, aci = tab_ref[4], tab_ref[5]
    pr = st_ref[0:1, :]
    pi = st_ref[1:2, :]
    sr = apr * cr - api * ci + acr * pr - aci * pi
    si = apr * ci + api * cr + acr * pi + aci * pr
    st_ref[0:1, :] = sr[L - 1:L, :]
    st_ref[1:2, :] = si[L - 1:L, :]
    y = (jnp.dot(sr.astype(BF16), ccat_ref[0:S5_W, :], preferred_element_type=F32)
         + jnp.dot(si.astype(BF16), ccat_ref[S5_W:2 * S5_W, :], preferred_element_type=F32))
    y = _gelu_tanh(y + u * d_ref[...])
    z = jnp.dot(y.astype(BF16), wglu_ref[...], preferred_element_type=F32)
    y = y * jax.nn.sigmoid(z)
    o_ref[...] = y * _rms_scale(y, -1)


def _s5_operands(lam_re, lam_im, log_dt, b_re, b_im, c_re, c_im):
    dt = jnp.exp(log_dt)[:, None]
    mag = jnp.exp(lam_re * dt)
    ar = mag * jnp.cos(lam_im * dt)
    ai = mag * jnp.sin(lam_im * dt)
    den = lam_re * lam_re + lam_im * lam_im
    fr = ((ar - 1.0) * lam_re + ai * lam_im) / den
    fi = (ai * lam_re - (ar - 1.0) * lam_im) / den
    bbr = fr[..., None] * b_re - fi[..., None] * b_im
    bbi = fr[..., None] * b_im + fi[..., None] * b_re
    eye = jnp.eye(S5_NGROUPS, dtype=F32)
    blk = lambda m: jnp.einsum('gph,gk->ghkp', m, eye).reshape(S5_CH, S5_W)
    bcat = jnp.concatenate([blk(bbr), blk(bbi)], axis=1).astype(BF16)
    blc = lambda m: jnp.einsum('ghp,gk->gpkh', m, eye).reshape(S5_W, S5_CH)
    ccat = jnp.concatenate([blc(c_re), -blc(c_im)], axis=0).astype(BF16)

    def powers(n):
        lr = (lam_re * dt).reshape(1, S5_W)
        li = (lam_im * dt).reshape(1, S5_W)
        m = jnp.exp(lr * n[:, None])
        return m * jnp.cos(li * n[:, None]), m * jnp.sin(li * n[:, None])

    j = jnp.arange(S5_CHUNK, dtype=F32)
    tabs = jnp.stack([*powers(-j), *powers(j), *powers(j + 1.0)], axis=0)
    return bcat, ccat, tabs


def _s5_mixer(ua, bsz, s, bcat, ccat, tabs, d_skip, w_glu):
    L = S5_CHUNK
    nc = s // L
    return pl.pallas_call(
        _s5_kernel,
        grid=(bsz, nc),
        in_specs=[pl.BlockSpec((L, S5_CH), lambda b, j: (b * nc + j, 0)),
                  pl.BlockSpec((S5_CH, 2 * S5_W), lambda b, j: (0, 0)),
                  pl.BlockSpec((2 * S5_W, S5_CH), lambda b, j: (0, 0)),
                  pl.BlockSpec((6, L, S5_W), lambda b, j: (0, 0, 0)),
                  pl.BlockSpec((1, S5_CH), lambda b, j: (0, 0)),
                  pl.BlockSpec((S5_CH, S5_CH), lambda b, j: (0, 0))],
        out_specs=pl.BlockSpec((L, S5_CH), lambda b, j: (b * nc + j, 0)),
        out_shape=jax.ShapeDtypeStruct((bsz * s, S5_CH), F32),
        scratch_shapes=[pltpu.VMEM((2, S5_W), F32)],
        compiler_params=_params(("parallel", "arbitrary")),
        name="s5_mixer",
    )(ua, bcat, ccat, tabs, d_skip, w_glu)


def _moba_prep_kernel(qkv_ref, gq_ref, gk_ref, cos_ref, sin_ref, qT_ref, k_ref, vT_ref, gate_ref, km_ref):
    j = pl.program_id(1)

    @pl.when(j == 0)
    def _():
        km_ref[...] = jnp.zeros_like(km_ref)

    x = qkv_ref[...]
    c = cos_ref[...]
    s = sin_ref[...]
    half = ROPE_DIMS // 2

    def prep(xT, g):
        y = xT * _rms_scale(xT, 0) * g
        x1 = y[0:half]
        x2 = y[half:ROPE_DIMS]
        return jnp.concatenate([x1 * c - x2 * s, x2 * c + x1 * s, y[ROPE_DIMS:]], axis=0)

    qT_all = x[:, 0:MOBA_W].T
    kT_all = x[:, MOBA_W:2 * MOBA_W].T
    vT_all = x[:, 2 * MOBA_W:3 * MOBA_W].T
    gq = gq_ref[...]
    gk = gk_ref[...]
    for h in range(MOBA_HEADS):
        sl = slice(h * HEAD_DIM, (h + 1) * HEAD_DIM)
        q = prep(qT_all[sl], gq) * (HEAD_DIM ** -0.5)
        qT_ref[h] = q.astype(BF16)
        gate_ref[h] = _dot3(km_ref[h], q)
        k = prep(kT_all[sl], gk).T
        k_ref[h] = k.astype(BF16)
        km_ref[h, pl.ds(j, 1), :] = jnp.mean(k, axis=0, keepdims=True)
        vT_ref[h] = vT_all[sl].astype(BF16)


def _moba_prep(qkv, bsz, s, gq, gk, cosT, sinT):
    nb = s // MOBA_BLOCK
    H, dh, blk = MOBA_HEADS, HEAD_DIM, MOBA_BLOCK
    return pl.pallas_call(
        _moba_prep_kernel,
        grid=(bsz, nb),
        in_specs=[pl.BlockSpec((blk, 3 * MOBA_W), lambda b, j: (b * nb + j, 0)),
                  pl.BlockSpec((dh, 1), lambda b, j: (0, 0)),
                  pl.BlockSpec((dh, 1), lambda b, j: (0, 0)),
                  pl.BlockSpec((ROPE_DIMS // 2, blk), lambda b, j: (0, j)),
                  pl.BlockSpec((ROPE_DIMS // 2, blk), lambda b, j: (0, j))],
        out_specs=[pl.BlockSpec((None, H, dh, blk), lambda b, j: (b, 0, 0, j)),
                   pl.BlockSpec((None, H, None, blk, dh), lambda b, j: (b, 0, j, 0, 0)),
                   pl.BlockSpec((None, H, None, dh, blk), lambda b, j: (b, 0, j, 0, 0)),
                   pl.BlockSpec((None, H, nb, blk), lambda b, j: (b, 0, 0, j))],
        out_shape=[jax.ShapeDtypeStruct((bsz, H, dh, s), BF16),
                   jax.ShapeDtypeStruct((bsz, H, nb, blk, dh), BF16),
                   jax.ShapeDtypeStruct((bsz, H, nb, dh, blk), BF16),
                   jax.ShapeDtypeStruct((bsz, H, nb, s), F32)],
        scratch_shapes=[pltpu.VMEM((H, nb, dh), F32)],
        compiler_params=_params(("parallel", "arbitrary")),
        name="moba_prep",
    )(qkv, gq, gk, cosT, sinT)


def _moba_attn_kernel(qT_ref, k_ref, vT_ref, gate_ref, o_ref, msk_ref):
    i = pl.program_id(2)
    nb = gate_ref.shape[0]
    blk = MOBA_BLOCK
    qT = qT_ref[...]
    gate = gate_ref[...]
    row = lax.broadcasted_iota(jnp.int32, (nb, blk), 0)
    gm = jnp.where(row < i, gate, NEG_INF)
    cnt = jnp.zeros((nb, blk), jnp.int32)
    for m in range(nb):
        gmm = gm[m:m + 1, :]
        beats = jnp.where(gmm > gm, 1, jnp.where(gmm == gm, jnp.where(row > m, 1, 0), 0))
        cnt = cnt + beats
    sel = jnp.where(row < i, jnp.where(cnt < MOBA_TOPK, 0.0, NEG_INF), NEG_INF)
    msk_ref[...] = sel

    kpos = lax.broadcasted_iota(jnp.int32, (blk, blk), 0)
    qpos = lax.broadcasted_iota(jnp.int32, (blk, blk), 1)
    sT = jnp.dot(k_ref[i], qT, preferred_element_type=F32)
    sT = jnp.where(kpos <= qpos, sT, NEG_INF)
    m0 = jnp.max(sT, axis=0, keepdims=True)
    p = jnp.exp(sT - m0)
    l0 = jnp.sum(p, axis=0, keepdims=True)
    acc0 = jnp.dot(vT_ref[i], p.astype(BF16), preferred_element_type=F32)

    U = MOBA_GROUP

    def body(g, carry):
        m, l, acc = carry
        blocks = [g * U + u for u in range(U)]
        sTs = [jnp.dot(k_ref[n], qT, preferred_element_type=F32) + msk_ref[pl.ds(n, 1), :] for n in blocks]
        m_new = m
        for sT in sTs:
            m_new = jnp.maximum(m_new, jnp.max(sT, axis=0, keepdims=True))
        alpha = jnp.exp(m - m_new)
        l = alpha * l
        acc = alpha * acc
        for n, sT in zip(blocks, sTs):
            p = jnp.exp(sT - m_new)
            l = l + jnp.sum(p, axis=0, keepdims=True)
            acc = acc + jnp.dot(vT_ref[n], p.astype(BF16), preferred_element_type=F32)
        return m_new, l, acc

    groups = lax.shift_right_logical(i + (U - 1), U.bit_length() - 1)
    _, l, acc = lax.fori_loop(0, groups, body, (m0, l0, acc0))
    o_ref[...] = acc / l


def _moba_attn(qT, k, vT, gate):
    bsz, H, dh, s = qT.shape
    nb = s // MOBA_BLOCK
    blk = MOBA_BLOCK
    return pl.pallas_call(
        _moba_attn_kernel,
        grid=(bsz, H, nb),
        in_specs=[pl.BlockSpec((None, None, dh, blk), lambda b, h, i: (b, h, 0, i)),
                  pl.BlockSpec((None, None, nb, blk, dh), lambda b, h, i: (b, h, 0, 0, 0)),
                  pl.BlockSpec((None, None, nb, dh, blk), lambda b, h, i: (b, h, 0, 0, 0)),
                  pl.BlockSpec((None, None, nb, blk), lambda b, h, i: (b, h, 0, i))],
        out_specs=pl.BlockSpec((None, None, dh, blk), lambda b, h, i: (b, h, 0, i)),
        out_shape=jax.ShapeDtypeStruct((bsz, H, dh, s), F32),
        scratch_shapes=[pltpu.VMEM((nb, blk), F32)],
        compiler_params=_params(("parallel", "parallel", "arbitrary")),
        name="moba_attn",
    )(qT, k, vT, gate)


def _ret_log_decay():
    return np.log(1.0 - 2.0 ** (-5.0 - np.arange(RET_HEADS, dtype=np.float64)))


def _ret_kernel(x_ref, cos_ref, sin_ref, xi_ref, zeta_ref, dm_ref, o_ref, st_ref):
    C = RET_CHUNK

    @pl.when(pl.program_id(1) == 0)
    def _():
        st_ref[...] = jnp.zeros_like(st_ref)

    x = x_ref[...]
    q = x[:, 0:RET_W]
    k = x[:, RET_W:2 * RET_W]
    v = x[:, 2 * RET_W:3 * RET_W]
    g = x[:, 3 * RET_W:4 * RET_W]
    cos = cos_ref[...]
    sin = sin_ref[...]
    lane = lax.broadcasted_iota(jnp.int32, (C, RET_W), 1)
    first = (lane & (HEAD_DIM // 2)) == 0

    def rot(t):
        swapped = jnp.where(first, pltpu.roll(t, RET_W - HEAD_DIM // 2, 1), pltpu.roll(t, HEAD_DIM // 2, 1))
        return t * cos + swapped * sin

    qr = rot(q)
    kr = rot(k) * (HEAD_DIM ** -0.5)
    qb = qr.astype(BF16)
    kb = kr.astype(BF16)
    vb = v.astype(BF16)
    qx = (qr * xi_ref[...]).astype(BF16)
    kz = kr * zeta_ref[...]
    decay_c = np.exp(C * _ret_log_decay())
    for h in range(RET_HEADS):
        sl = slice(h * HEAD_DIM, (h + 1) * HEAD_DIM)
        sc = lax.dot_general(qb[:, sl], kb[:, sl], (((1,), (1,)), ((), ())),
                             preferred_element_type=F32) * dm_ref[h]
        inner = jnp.dot(sc.astype(BF16), vb[:, sl], preferred_element_type=F32)
        r_prev = st_ref[h]
        cross = jnp.dot(qx[:, sl], r_prev.astype(BF16), preferred_element_type=F32)
        kv = jnp.dot(kz[:, sl].T.astype(BF16), vb[:, sl], preferred_element_type=F32)
        st_ref[h] = float(decay_c[h]) * r_prev + kv
        o = inner + cross
        o = o * _rms_scale(o, -1)
        gh = g[:, sl]
        o_ref[:, sl] = gh * jax.nn.sigmoid(gh) * o


def _ret_tables(s):
    C = RET_CHUNK
    half = HEAD_DIM // 2
    pos = jnp.arange(s, dtype=F32)
    inv = RET_ANGLE_BASE ** (-jnp.linspace(0.0, 1.0, half, dtype=F32))
    ang = pos[:, None] * inv[None, :]
    cos, sin = jnp.cos(ang), jnp.sin(ang)
    cos_t = jnp.tile(cos, (1, 2 * RET_HEADS))
    sin_t = jnp.tile(jnp.concatenate([-sin, sin], axis=1), (1, RET_HEADS))
    lg = _ret_log_decay()
    i = np.arange(C, dtype=np.float64)
    xi = np.repeat(np.exp((i + 1.0)[:, None] * lg[None, :]), HEAD_DIM, axis=1)
    zeta = np.repeat(np.exp((C - 1.0 - i)[:, None] * lg[None, :]), HEAD_DIM, axis=1)
    diff = i[:, None] - i[None, :]
    dm = np.where(diff >= 0, np.exp(np.maximum(diff, 0.0)[None] * lg[:, None, None]), 0.0)
    return cos_t, sin_t, jnp.asarray(xi, F32), jnp.asarray(zeta, F32), jnp.asarray(dm, F32)


def _ret_mixer(rt, bsz, s, tables):
    C = RET_CHUNK
    nc = s // C
    cos_t, sin_t, xi, zeta, dm = tables
    return pl.pallas_call(
        _ret_kernel,
        grid=(bsz, nc),
        in_specs=[pl.BlockSpec((C, 4 * RET_W), lambda b, j: (b * nc + j, 0)),
                  pl.BlockSpec((C, RET_W), lambda b, j: (j, 0)),
                  pl.BlockSpec((C, RET_W), lambda b, j: (j, 0)),
                  pl.BlockSpec((C, RET_W), lambda b, j: (0, 0)),
                  pl.BlockSpec((C, RET_W), lambda b, j: (0, 0)),
                  pl.BlockSpec((RET_HEADS, C, C), lambda b, j: (0, 0, 0))],
        out_specs=pl.BlockSpec((C, RET_W), lambda b, j: (b * nc + j, 0)),
        out_shape=jax.ShapeDtypeStruct((bsz * s, RET_W), F32),
        scratch_shapes=[pltpu.VMEM((RET_HEADS, HEAD_DIM, HEAD_DIM), F32)],
        compiler_params=_params(("parallel", "arbitrary")),
        name="ret_mixer",
    )(rt, cos_t, sin_t, xi, zeta, dm)


def _out_proj_kernel(x_ref, ya_ref, ot_ref, yc_ref, g_ref, w_ref, o_ref):
    ts = x_ref.shape[0]
    ot = ot_ref[...]
    yb = (ot * _rms_scale(ot, 1)).reshape(MOBA_W, ts).T
    g = g_ref[...]
    a0, a1 = S5_CH, S5_CH + MOBA_W
    acc = x_ref[...]
    acc = acc + jnp.dot((ya_ref[...] * g[:, 0:a0]).astype(BF16), w_ref[0:a0, :], preferred_element_type=F32)
    acc = acc + jnp.dot((yb * g[:, a0:a1]).astype(BF16), w_ref[a0:a1, :], preferred_element_type=F32)
    acc = acc + jnp.dot((yc_ref[...] * g[:, a1:MIX_W]).astype(BF16), w_ref[a1:MIX_W, :],
                        preferred_element_type=F32)
    o_ref[...] = acc


def _out_proj(x2, ya, ot, yc, g, w, bsz, s):
    ts = PROJ_ROWS
    nt = s // ts
    return pl.pallas_call(
        _out_proj_kernel,
        grid=(bsz, nt),
        in_specs=[pl.BlockSpec((ts, D_MODEL), lambda b, j: (b * nt + j, 0)),
                  pl.BlockSpec((ts, S5_CH), lambda b, j: (b * nt + j, 0)),
                  pl.BlockSpec((None, MOBA_HEADS, HEAD_DIM, ts), lambda b, j: (b, 0, 0, j)),
                  pl.BlockSpec((ts, RET_W), lambda b, j: (b * nt + j, 0)),
                  pl.BlockSpec((1, MIX_W), lambda b, j: (0, 0)),
                  pl.BlockSpec((MIX_W, D_MODEL), lambda b, j: (0, 0))],
        out_specs=pl.BlockSpec((ts, D_MODEL), lambda b, j: (b * nt + j, 0)),
        out_shape=jax.ShapeDtypeStruct((bsz * s, D_MODEL), F32),
        compiler_params=_params(("parallel", "parallel")),
        name="out_proj",
    )(x2, ya, ot, yc, g, w)


def _top16(s, rank, payload=None):
    vals, pays = [], []
    for _ in range(PEER_TOPK):
        m = jnp.max(s, axis=0, keepdims=True)
        pos = jnp.min(jnp.where(s == m, rank, jnp.int32(2 ** 30)), axis=0, keepdims=True)
        hit = rank == pos
        vals.append(m)
        pays.append(pos if payload is None else jnp.sum(jnp.where(hit, payload, 0), axis=0, keepdims=True))
        s = jnp.where(hit, NEG_INF, s)
    return jnp.concatenate(vals, axis=0), jnp.concatenate(pays, axis=0)


def _pair_candidates(v0, i0, v1, i1):
    K, n = PEER_TOPK, v0.shape[1]
    j = lax.broadcasted_iota(jnp.int32, (8, n), 0)
    vals, flat, eid = [], [], []
    for a in range(4):
        vals.append(v0[a:a + 1] + v1[0:8])
        flat.append(j + a * K)
        eid.append(i0[a:a + 1] * PEER_KEYS + i1[0:8])
    vals.append(v0[0:1] + v1[8:16])
    flat.append(j + 8)
    eid.append(i0[0:1] * PEER_KEYS + i1[8:16])
    vals.append(v0[8:16] + v1[0:1])
    flat.append((j + 8) * K)
    eid.append(i0[8:16] * PEER_KEYS + i1[0:1])
    for b in range(3):
        vals.append(jnp.where(j >= 4, v0[0:8] + v1[b:b + 1], NEG_INF))
        flat.append(jnp.where(j >= 4, j * K + b, K * K + j * K + b))
        eid.append(i0[0:8] * PEER_KEYS + i1[b:b + 1])
    return jnp.concatenate(vals, axis=0), jnp.concatenate(flat, axis=0), jnp.concatenate(eid, axis=0)


def _peer_route_kernel(x_ref, g_ref, wqT_ref, sk_ref, h2_ref, idx_ref, gate_ref, q_scr):
    tb = x_ref.shape[0]
    x = x_ref[...]
    h2 = x * _rms_scale(x, -1) * g_ref[...]
    h2_ref[...] = h2
    q_scr[...] = lax.dot_general(wqT_ref[...], h2.astype(BF16), (((1,), (1,)), ((), ())),
                                 preferred_element_type=F32).astype(BF16)
    half = PEER_QDIM // 2
    io_keys = lax.broadcasted_iota(jnp.int32, (PEER_KEYS, tb), 0)

    def head(h, carry):
        sub = []
        for c in range(2):
            grp = 2 * h + c
            qg = q_scr[pl.ds(pl.multiple_of(grp * half, half), half), :]
            sc = jnp.dot(sk_ref[grp], qg, preferred_element_type=F32)
            sub.append(_top16(sc, io_keys))
        (v0, i0), (v1, i1) = sub
        fs, eidx = _top16(*_pair_candidates(v0, i0, v1, i1))
        e = jnp.exp(fs - fs[0:1, :])
        gates = e / jnp.sum(e, axis=0, keepdims=True)
        r0 = pl.multiple_of(h * PEER_TOPK, PEER_TOPK)
        idx_ref[pl.ds(r0, PEER_TOPK), :] = eidx
        gate_ref[pl.ds(r0, PEER_TOPK), :] = gates
        return carry

    lax.fori_loop(0, PEER_HEADS, head, 0)


def _peer_route(x2, g, wqT, sk):
    t = x2.shape[0]
    tb = PEER_ROUTE_TOKENS
    ngrp = 2 * PEER_HEADS
    return pl.pallas_call(
        _peer_route_kernel,
        grid=(t // tb,),
        in_specs=[pl.BlockSpec((tb, D_MODEL), lambda i: (i, 0)),
                  pl.BlockSpec((1, D_MODEL), lambda i: (0, 0)),
                  pl.BlockSpec((PEER_HEADS * PEER_QDIM, D_MODEL), lambda i: (0, 0)),
                  pl.BlockSpec((ngrp, PEER_KEYS, PEER_QDIM // 2), lambda i: (0, 0, 0))],
        out_specs=[pl.BlockSpec((tb, D_MODEL), lambda i: (i, 0)),
                   pl.BlockSpec((PEER_SEL, tb), lambda i: (0, i)),
                   pl.BlockSpec((PEER_SEL, tb), lambda i: (0, i))],
        out_shape=[jax.ShapeDtypeStruct((t, D_MODEL), F32),
                   jax.ShapeDtypeStruct((PEER_SEL, t), jnp.int32),
                   jax.ShapeDtypeStruct((PEER_SEL, t), F32)],
        scratch_shapes=[pltpu.VMEM((PEER_HEADS * PEER_QDIM, tb), BF16)],
        compiler_params=_params(("parallel",)),
        name="peer_route",
    )(x2, g, wqT, sk)


def _pack_table(tab):
    bits = lax.bitcast_convert_type(tab.astype(BF16), jnp.uint16).astype(jnp.uint32)
    half = D_MODEL // 2
    words = bits[:, :half] | (bits[:, half:] << 16)
    return lax.bitcast_convert_type(words, jnp.int32).reshape(tab.shape[0] * TABLE_ROWS_PER_EXPERT, LANES)


def _table_spec(tab):
    return pl.BlockSpec(tab.shape, lambda i: (0, 0), pipeline_mode=pl.Buffered(1))


def _gather_rows(idx_ref, t, tab_ref, tile_ref):
    R = TABLE_ROWS_PER_EXPERT
    for k in range(PEER_SEL):
        row = pl.multiple_of(idx_ref[t, k], R)
        tile_ref[pl.ds(k, R, stride=TILE_STRIDE), :] = tab_ref[pl.ds(row, R), :]


def _tile_chunk(tile_ref, r):
    return pltpu.bitcast(tile_ref[r * TILE_STRIDE:r * TILE_STRIDE + PEER_SEL, :], BF16)


def _pipelined_tokens(tb, idx_ref, tab_ref, tiles, compute, finish, init):
    tile_a, tile_b = tiles
    _gather_rows(idx_ref, 0, tab_ref, tile_a)
    _gather_rows(idx_ref, 1, tab_ref, tile_b)

    def pair(i, pending):
        t0 = 2 * i
        finish(jnp.maximum(t0 - 1, 0), pending)
        va = compute(t0, tile_a)
        _gather_rows(idx_ref, jnp.minimum(t0 + 2, tb - 1), tab_ref, tile_a)
        finish(t0, va)
        vb = compute(t0 + 1, tile_b)
        _gather_rows(idx_ref, jnp.minimum(t0 + 3, tb - 1), tab_ref, tile_b)
        return vb

    last = lax.fori_loop(0, tb // 2, pair, init)
    finish(tb - 1, last)


def _hi_lo_rows(v, lane_parity):
    n = v.shape[1]
    vb = jnp.broadcast_to(v, (8, n))
    hi = vb.astype(BF16).astype(F32)
    srow = lax.broadcasted_iota(jnp.int32, (8, n), 0)
    part = jnp.where(srow < 2, hi, vb - hi)
    part = jnp.where(srow < 4, part, 0.0)
    return jnp.where((srow & 1) == lane_parity, part, 0.0)


def _peer_act_kernel(idx_ref, x_ref, gate_ref, tab_ref, c_ref, tile_a, tile_b):
    tb = x_ref.shape[0]
    R = TABLE_ROWS_PER_EXPERT
    half = D_MODEL // 2
    lane = lax.broadcasted_iota(jnp.int32, (PEER_SEL, tb), 1)
    c_ref[...] = jnp.zeros_like(c_ref)

    def compute(t, tile_ref):
        x = x_ref[pl.ds(t, 1), :]
        acc = jnp.zeros((PEER_SEL, LANES), F32)
        for r in range(R):
            w = tile_ref[r * TILE_STRIDE:r * TILE_STRIDE + PEER_SEL, :]
            lo = lax.bitcast_convert_type(w << 16, F32)
            hi = lax.bitcast_convert_type(w & jnp.int32(-65536), F32)
            acc = acc + lo * x[:, r * LANES:(r + 1) * LANES]
            acc = acc + hi * x[:, half + r * LANES:half + (r + 1) * LANES]
        return jnp.sum(acc, axis=1, keepdims=True)

    def finish(t, col):
        c_ref[...] = jnp.where(lane == t, col, c_ref[...])

    _pipelined_tokens(tb, idx_ref, tab_ref, (tile_a, tile_b), compute, finish,
                      jnp.zeros((PEER_SEL, 1), F32))
    c_ref[...] = gate_ref[...] * _gelu_tanh(c_ref[...])


def _tile_scratch():
    shape = (TABLE_ROWS_PER_EXPERT * TILE_STRIDE, LANES)
    return [pltpu.VMEM(shape, jnp.int32), pltpu.VMEM(shape, jnp.int32)]


def _peer_act(idx, h2, gate_t, tab):
    t = h2.shape[0]
    tb = PEER_GATHER_TOKENS
    return pl.pallas_call(
        _peer_act_kernel,
        grid=(t // tb,),
        in_specs=[pl.BlockSpec((tb, PEER_SEL), lambda i: (i, 0), memory_space=pltpu.SMEM),
                  pl.BlockSpec((tb, D_MODEL), lambda i: (i, 0)),
                  pl.BlockSpec((PEER_SEL, tb), lambda i: (0, i)),
                  _table_spec(tab)],
        out_specs=pl.BlockSpec((PEER_SEL, tb), lambda i: (0, i)),
        out_shape=jax.ShapeDtypeStruct((PEER_SEL, t), F32),
        scratch_shapes=_tile_scratch(),
        compiler_params=_params(("arbitrary",)),
        name="peer_act",
    )(idx, h2, gate_t, tab)


def _peer_out_kernel(idx_ref, c_ref, tab_ref, o_ref, tile_a, tile_b):
    tb = o_ref.shape[0]
    R = TABLE_ROWS_PER_EXPERT
    half = D_MODEL // 2
    lane_parity = lax.broadcasted_iota(jnp.int32, (8, 2 * PEER_SEL), 1) & 1

    def compute(t, tile_ref):
        rows = _hi_lo_rows(c_ref[t], lane_parity).astype(BF16)
        res = [jnp.dot(rows, _tile_chunk(tile_ref, r), preferred_element_type=F32) for r in range(R)]
        first = jnp.concatenate([v[0:1] + v[2:3] for v in res], axis=1)
        second = jnp.concatenate([v[1:2] + v[3:4] for v in res], axis=1)
        return jnp.concatenate([first, second], axis=1)

    def finish(t, row):
        o_ref[t] = row

    _pipelined_tokens(tb, idx_ref, tab_ref, (tile_a, tile_b), compute, finish,
                      jnp.zeros((1, D_MODEL), F32))


def _peer_out(idx, c_i, tab):
    t = idx.shape[0]
    tb = PEER_GATHER_TOKENS
    return pl.pallas_call(
        _peer_out_kernel,
        grid=(t // tb,),
        in_specs=[pl.BlockSpec((tb, PEER_SEL), lambda i: (i, 0), memory_space=pltpu.SMEM),
                  pl.BlockSpec((tb, 1, 2 * PEER_SEL), lambda i: (i, 0, 0)),
                  _table_spec(tab)],
        out_specs=pl.BlockSpec((tb, 1, D_MODEL), lambda i: (i, 0, 0)),
        out_shape=jax.ShapeDtypeStruct((t, 1, D_MODEL), F32),
        scratch_shapes=_tile_scratch(),
        compiler_params=_params(("arbitrary",)),
        name="peer_out",
    )(idx, c_i, tab)


def _mixer_layer(x2, bsz, s, p, rope, ret_tables):
    ua, mb, rt = _in_proj(x2, p['norm1_g'], p['w_in'], p['w_in_qk_lo'])
    ya = _s5_mixer(ua, bsz, s, p['s5_bcat'], p['s5_ccat'], p['s5_tabs'], p['s5_d'], p['s5_w_glu'])
    qT, k, vT, gate = _moba_prep(mb, bsz, s, p['moba_q_g'], p['moba_k_g'], *rope)
    ot = _moba_attn(qT, k, vT, gate)
    yc = _ret_mixer(rt, bsz, s, ret_tables)
    return _out_proj(x2, ya, ot, yc, p['mix_out_g'], p['w_out'], bsz, s)


def _peer_layer(x2, p):
    h2, idx_t, gate_t = _peer_route(x2, p['norm2_g'], p['peer_wqT'], p['peer_sk'])
    idx = idx_t.T * TABLE_ROWS_PER_EXPERT
    c_t = _peer_act(idx, h2, gate_t, p['peer_u'])
    c_i = jnp.repeat(c_t.T, 2, axis=1)[:, None, :]
    return x2 + _peer_out(idx, c_i, p['peer_v']).reshape(x2.shape)


def kernel(x, norm1_g, w_in, s5_lam_re, s5_lam_im, s5_log_dt, s5_b_re, s5_b_im, s5_c_re, s5_c_im,
           s5_d, s5_w_glu, moba_q_g, moba_k_g, mix_out_g, w_out, norm2_g, peer_w_q, peer_sub_keys,
           peer_u, peer_v):
    bsz, s, d = x.shape
    depth = w_in.shape[0]
    pos = jnp.arange(s, dtype=F32)
    inv = ROPE_THETA ** (-jnp.arange(0, ROPE_DIMS, 2, dtype=F32) / ROPE_DIMS)
    ang = (pos[:, None] * inv[None, :]).T
    rope = (jnp.cos(ang), jnp.sin(ang))
    ret_tables = _ret_tables(s)
    x2 = x.reshape(bsz * s, d)
    for l in range(depth):
        bcat, ccat, tabs = _s5_operands(s5_lam_re[l], s5_lam_im[l], s5_log_dt[l], s5_b_re[l],
                                        s5_b_im[l], s5_c_re[l], s5_c_im[l])
        w_in_hi = w_in[l].astype(BF16)
        qk_cols = slice(S5_CH, S5_CH + 2 * MOBA_W)
        p = {
            'norm1_g': norm1_g[l][None, :],
            'w_in': w_in_hi,
            'w_in_qk_lo': (w_in[l][:, qk_cols] - w_in_hi[:, qk_cols].astype(F32)).astype(BF16),
            's5_bcat': bcat, 's5_ccat': ccat, 's5_tabs': tabs,
            's5_d': s5_d[l][None, :],
            's5_w_glu': s5_w_glu[l].astype(BF16),
            'moba_q_g': moba_q_g[l][:, None],
            'moba_k_g': moba_k_g[l][:, None],
            'mix_out_g': mix_out_g[l][None, :],
            'w_out': w_out[l].astype(BF16),
            'norm2_g': norm2_g[l][None, :],
            'peer_wqT': peer_w_q[l].T.astype(BF16),
            'peer_sk': peer_sub_keys[l].reshape(2 * PEER_HEADS, PEER_KEYS, PEER_QDIM // 2).astype(BF16),
            'peer_u': _pack_table(peer_u[l]),
            'peer_v': _pack_table(peer_v[l]),
        }
        x2 = _mixer_layer(x2, bsz, s, p, rope, ret_tables)
        x2 = _peer_layer(x2, p)
    return x2.reshape(bsz, s, d)
```

```python
import functools
import math

import numpy as np
import jax
import jax.numpy as jnp
from jax import lax
from jax.experimental import pallas as pl
from jax.experimental.pallas import tpu as pltpu

D_MODEL = 1024
HEAD_DIM = 64
S5_CH = 256
S5_GROUP = 16
S5_NGROUPS = 16
S5_STATE = 64
S5_W = S5_NGROUPS * S5_STATE
MOBA_HEADS = 8
MOBA_W = 512
MOBA_BLOCK = 256
MOBA_TOPK = 3
MOBA_GROUP = 4
MOBA_HEADS_PER_STEP = 2
ROPE_THETA = 500000.0
ROPE_DIMS = 16
RET_HEADS = 4
RET_W = 256
RET_ANGLE_BASE = 10000.0
MIX_W = S5_CH + MOBA_W + RET_W
IN_W = S5_CH + 3 * MOBA_W + 4 * RET_W
PEER_KEYS = 128
PEER_HEADS = 8
PEER_TOPK = 16
PEER_QDIM = 128
PEER_SEL = PEER_HEADS * PEER_TOPK
EPS = 1e-6

LANES = 128
TABLE_ROWS_PER_EXPERT = D_MODEL // (2 * LANES)
TILE_STRIDE = 136
VMEM_LIMIT = 56 * 1024 * 1024

S5_CHUNK = 128
RET_CHUNK = 256
PROJ_ROWS = 256
PEER_ROUTE_TOKENS = 256
PEER_GATHER_TOKENS = 128

BF16 = jnp.bfloat16
F32 = jnp.float32
NEG_INF = float("-inf")


def _params(sem):
    return pltpu.CompilerParams(dimension_semantics=sem, vmem_limit_bytes=VMEM_LIMIT)


def _gelu_tanh(y):
    return 0.5 * y * (1.0 + jnp.tanh(0.7978845608028654 * (y + 0.044715 * (y * y * y))))


def _rms_scale(x, axis):
    return lax.rsqrt(jnp.mean(x * x, axis=axis, keepdims=True) + EPS)


def _split(a):
    hi = a.astype(BF16)
    return hi, (a - hi.astype(F32)).astype(BF16)


def _dot3(a, b):
    a_hi, a_lo = _split(a)
    b_hi, b_lo = _split(b)
    out = jnp.dot(a_hi, b_hi, preferred_element_type=F32)
    out = out + jnp.dot(a_hi, b_lo, preferred_element_type=F32)
    return out + jnp.dot(a_lo, b_hi, preferred_element_type=F32)


def _in_proj_kernel(x_ref, g_ref, w_ref, wlo_ref, ua_ref, mb_ref, rt_ref):
    x = x_ref[...]
    hf = x * _rms_scale(x, -1) * g_ref[...]
    h = hf.astype(BF16)
    ua_ref[...] = jnp.dot(h, w_ref[:, 0:S5_CH], preferred_element_type=F32)
    h_lo = (hf - h.astype(F32)).astype(BF16)
    q0, q1 = S5_CH, S5_CH + 2 * MOBA_W
    qk = jnp.dot(h, w_ref[:, q0:q1], preferred_element_type=F32)
    qk = qk + jnp.dot(h, wlo_ref[...], preferred_element_type=F32)
    qk = qk + jnp.dot(h_lo, w_ref[:, q0:q1], preferred_element_type=F32)
    mb_ref[:, 0:2 * MOBA_W] = qk
    mb_ref[:, 2 * MOBA_W:3 * MOBA_W] = jnp.dot(h, w_ref[:, q1:q1 + MOBA_W], preferred_element_type=F32)
    rt_ref[...] = jnp.dot(h, w_ref[:, S5_CH + 3 * MOBA_W:IN_W], preferred_element_type=F32)


def _in_proj(x2, g, w, wlo):
    t = x2.shape[0]
    tm = PROJ_ROWS
    return pl.pallas_call(
        _in_proj_kernel,
        grid=(t // tm,),
        in_specs=[pl.BlockSpec((tm, D_MODEL), lambda i: (i, 0)),
                  pl.BlockSpec((1, D_MODEL), lambda i: (0, 0)),
                  pl.BlockSpec((D_MODEL, IN_W), lambda i: (0, 0)),
                  pl.BlockSpec((D_MODEL, 2 * MOBA_W), lambda i: (0, 0))],
        out_specs=[pl.BlockSpec((tm, S5_CH), lambda i: (i, 0)),
                   pl.BlockSpec((tm, 3 * MOBA_W), lambda i: (i, 0)),
                   pl.BlockSpec((tm, 4 * RET_W), lambda i: (i, 0))],
        out_shape=[jax.ShapeDtypeStruct((t, S5_CH), F32),
                   jax.ShapeDtypeStruct((t, 3 * MOBA_W), F32),
                   jax.ShapeDtypeStruct((t, 4 * RET_W), F32)],
        compiler_params=_params(("parallel",)),
        name="in_proj",
    )(x2, g, w, wlo)


def _s5_kernel(u_ref, bcat_ref, ccat_ref, tab_ref, d_ref, wglu_ref, o_ref, st_ref):
    L = S5_CHUNK

    @pl.when(pl.program_id(1) == 0)
    def _():
        st_ref[...] = jnp.zeros_like(st_ref)

    u = u_ref[...]
    bu = jnp.dot(u.astype(BF16), bcat_ref[...], preferred_element_type=F32)
    bur, bui = bu[:, :S5_W], bu[:, S5_W:]
    air, aii = tab_ref[0], tab_ref[1]
    ktr = bur * air - bui * aii
    kti = bur * aii + bui * air
    row = lax.broadcasted_iota(jnp.int32, (L, L), 0)
    col = lax.broadcasted_iota(jnp.int32, (L, L), 1)
    tri = jnp.where(row >= col, 1.0, 0.0).astype(BF16)
    cr = jnp.dot(tri, ktr.astype(BF16), preferred_element_type=F32)
    ci = jnp.dot(tri, kti.astype(BF16), preferred_element_type=F32)
    apr, api = tab_ref[2], tab_ref[3]
    acr, aci = tab_ref[4], tab_ref[5]
    pr = st_ref[0:1, :]
    pi = st_ref[1:2, :]
    sr = apr * cr - api * ci + acr * pr - aci * pi
    si = apr * ci + api * cr + acr * pi + aci * pr
    st_ref[0:1, :] = sr[L - 1:L, :]
    st_ref[1:2, :] = si[L - 1:L, :]
    y = (jnp.dot(sr.astype(BF16), ccat_ref[0:S5_W, :], preferred_element_type=F32)
         + jnp.dot(si.astype(BF16), ccat_ref[S5_W:2 * S5_W, :], preferred_element_type=F32))
    y = _gelu_tanh(y + u * d_ref[...])
    z = jnp.dot(y.astype(BF16), wglu_ref[...], preferred_element_type=F32)
    y = y * jax.nn.sigmoid(z)
    o_ref[...] = y * _rms_scale(y, -1)


def _s5_operands(lam_re, lam_im, log_dt, b_re, b_im, c_re, c_im):
    dt = jnp.exp(log_dt)[:, None]
    mag = jnp.exp(lam_re * dt)
    ar = mag * jnp.cos(lam_im * dt)
    ai = mag * jnp.sin(lam_im * dt)
    den = lam_re * lam_re + lam_im * lam_im
    fr = ((ar - 1.0) * lam_re + ai * lam_im) / den
    fi = (ai * lam_re - (ar - 1.0) * lam_im) / den
    bbr = fr[..., None] * b_re - fi[..., None] * b_im
    bbi = fr[..., None] * b_im + fi[..., None] * b_re
    eye = jnp.eye(S5_NGROUPS, dtype=F32)
    blk = lambda m: jnp.einsum('gph,gk->ghkp', m, eye).reshape(S5_CH, S5_W)
    bcat = jnp.concatenate([blk(bbr), blk(bbi)], axis=1).astype(BF16)
    blc = lambda m: jnp.einsum('ghp,gk->gpkh', m, eye).reshape(S5_W, S5_CH)
    ccat = jnp.concatenate([blc(c_re), -blc(c_im)], axis=0).astype(BF16)

    def powers(n):
        lr = (lam_re * dt).reshape(1, S5_W)
        li = (lam_im * dt).reshape(1, S5_W)
        m = jnp.exp(lr * n[:, None])
        return m * jnp.cos(li * n[:, None]), m * jnp.sin(li * n[:, None])

    j = jnp.arange(S5_CHUNK, dtype=F32)
    tabs = jnp.stack([*powers(-j), *powers(j), *powers(j + 1.0)], axis=0)
    return bcat, ccat, tabs


def _s5_mixer(ua, bsz, s, bcat, ccat, tabs, d_skip, w_glu):
    L = S5_CHUNK
    nc = s // L
    return pl.pallas_call(
        _s5_kernel,
        grid=(bsz, nc),
        in_specs=[pl.BlockSpec((L, S5_CH), lambda b, j: (b * nc + j, 0)),
                  pl.BlockSpec((S5_CH, 2 * S5_W), lambda b, j: (0, 0)),
                  pl.BlockSpec((2 * S5_W, S5_CH), lambda b, j: (0, 0)),
                  pl.BlockSpec((6, L, S5_W), lambda b, j: (0, 0, 0)),
                  pl.BlockSpec((1, S5_CH), lambda b, j: (0, 0)),
                  pl.BlockSpec((S5_CH, S5_CH), lambda b, j: (0, 0))],
        out_specs=pl.BlockSpec((L, S5_CH), lambda b, j: (b * nc + j, 0)),
        out_shape=jax.ShapeDtypeStruct((bsz * s, S5_CH), F32),
        scratch_shapes=[pltpu.VMEM((2, S5_W), F32)],
        compiler_params=_params(("parallel", "arbitrary")),
        name="s5_mixer",
    )(ua, bcat, ccat, tabs, d_skip, w_glu)


def _moba_prep_kernel(qkv_ref, gq_ref, gk_ref, cos_ref, sin_ref, qT_ref, k_ref, vT_ref, gate_ref, km_ref):
    j = pl.program_id(1)

    @pl.when(j == 0)
    def _():
        km_ref[...] = jnp.zeros_like(km_ref)

    x = qkv_ref[...]
    c = cos_ref[...]
    s = sin_ref[...]
    half = ROPE_DIMS // 2

    def prep(xT, g):
        y = xT * _rms_scale(xT, 0) * g
        x1 = y[0:half]
        x2 = y[half:ROPE_DIMS]
        return jnp.concatenate([x1 * c - x2 * s, x2 * c + x1 * s, y[ROPE_DIMS:]], axis=0)

    qT_all = x[:, 0:MOBA_W].T
    kT_all = x[:, MOBA_W:2 * MOBA_W].T
    vT_all = x[:, 2 * MOBA_W:3 * MOBA_W].T
    gq = gq_ref[...]
    gk = gk_ref[...]
    for h in range(MOBA_HEADS):
        sl = slice(h * HEAD_DIM, (h + 1) * HEAD_DIM)
        q = prep(qT_all[sl], gq) * (HEAD_DIM ** -0.5)
        qT_ref[h] = q.astype(BF16)
        gate_ref[h] = _dot3(km_ref[h], q)
        k = prep(kT_all[sl], gk).T
        k_ref[h] = k.astype(BF16)
        km_ref[h, pl.ds(j, 1), :] = jnp.mean(k, axis=0, keepdims=True)
        vT_ref[h] = vT_all[sl].astype(BF16)


def _moba_prep(qkv, bsz, s, gq, gk, cosT, sinT):
    nb = s // MOBA_BLOCK
    H, dh, blk = MOBA_HEADS, HEAD_DIM, MOBA_BLOCK
    return pl.pallas_call(
        _moba_prep_kernel,
        grid=(bsz, nb),
        in_specs=[pl.BlockSpec((blk, 3 * MOBA_W), lambda b, j: (b * nb + j, 0)),
                  pl.BlockSpec((dh, 1), lambda b, j: (0, 0)),
                  pl.BlockSpec((dh, 1), lambda b, j: (0, 0)),
                  pl.BlockSpec((ROPE_DIMS // 2, blk), lambda b, j: (0, j)),
                  pl.BlockSpec((ROPE_DIMS // 2, blk), lambda b, j: (0, j))],
        out_specs=[pl.BlockSpec((None, H, dh, blk), lambda b, j: (b, 0, 0, j)),
                   pl.BlockSpec((None, H, None, blk, dh), lambda b, j: (b, 0, j, 0, 0)),
                   pl.BlockSpec((None, H, None, dh, blk), lambda b, j: (b, 0, j, 0, 0)),
                   pl.BlockSpec((None, H, nb, blk), lambda b, j: (b, 0, 0, j))],
        out_shape=[jax.ShapeDtypeStruct((bsz, H, dh, s), BF16),
                   jax.ShapeDtypeStruct((bsz, H, nb, blk, dh), BF16),
                   jax.ShapeDtypeStruct((bsz, H, nb, dh, blk), BF16),
                   jax.ShapeDtypeStruct((bsz, H, nb, s), F32)],
        scratch_shapes=[pltpu.VMEM((H, nb, dh), F32)],
        compiler_params=_params(("parallel", "arbitrary")),
        name="moba_prep",
    )(qkv, gq, gk, cosT, sinT)


def _moba_attn_kernel(qT_ref, k_ref, vT_ref, gate_ref, o_ref, msk_ref):
    i = pl.program_id(2)
    hp, nb = gate_ref.shape[0], gate_ref.shape[1]
    blk = MOBA_BLOCK
    row = lax.broadcasted_iota(jnp.int32, (nb, blk), 0)
    kpos = lax.broadcasted_iota(jnp.int32, (blk, blk), 0)
    qpos = lax.broadcasted_iota(jnp.int32, (blk, blk), 1)
    qTs, init = [], []
    for h in range(hp):
        qT = qT_ref[h]
        gm = jnp.where(row < i, gate_ref[h], NEG_INF)
        cnt = jnp.zeros((nb, blk), jnp.int32)
        for m in range(nb):
            gmm = gm[m:m + 1, :]
            beats = jnp.where(gmm > gm, 1, jnp.where(gmm == gm, jnp.where(row > m, 1, 0), 0))
            cnt = cnt + beats
        msk_ref[h] = jnp.where(row < i, jnp.where(cnt < MOBA_TOPK, 0.0, NEG_INF), NEG_INF)

        sT = jnp.dot(k_ref[h, i], qT, preferred_element_type=F32)
        sT = jnp.where(kpos <= qpos, sT, NEG_INF)
        m0 = jnp.max(sT, axis=0, keepdims=True)
        p = jnp.exp(sT - m0)
        l0 = jnp.sum(p, axis=0, keepdims=True)
        acc0 = jnp.dot(vT_ref[h, i], p.astype(BF16), preferred_element_type=F32)
        qTs.append(qT)
        init.append((m0, l0, acc0))

    U = MOBA_GROUP

    def body(g, carry):
        blocks = [g * U + u for u in range(U)]
        out = []
        for h in range(hp):
            m, l, acc = carry[h]
            sTs = [jnp.dot(k_ref[h, n], qTs[h], preferred_element_type=F32) + msk_ref[h, pl.ds(n, 1), :]
                   for n in blocks]
            m_new = m
            for sT in sTs:
                m_new = jnp.maximum(m_new, jnp.max(sT, axis=0, keepdims=True))
            alpha = jnp.exp(m - m_new)
            l = alpha * l
            acc = alpha * acc
            for n, sT in zip(blocks, sTs):
                p = jnp.exp(sT - m_new)
                l = l + jnp.sum(p, axis=0, keepdims=True)
                acc = acc + jnp.dot(vT_ref[h, n], p.astype(BF16), preferred_element_type=F32)
            out.append((m_new, l, acc))
        return tuple(out)

    groups = lax.shift_right_logical(i + (U - 1), U.bit_length() - 1)
    final = lax.fori_loop(0, groups, body, tuple(init))
    for h in range(hp):
        _, l, acc = final[h]
        o_ref[h] = acc / l


def _moba_attn(qT, k, vT, gate):
    bsz, H, dh, s = qT.shape
    nb = s // MOBA_BLOCK
    blk = MOBA_BLOCK
    hp = MOBA_HEADS_PER_STEP
    return pl.pallas_call(
        _moba_attn_kernel,
        grid=(bsz, H // hp, nb),
        in_specs=[pl.BlockSpec((None, hp, dh, blk), lambda b, h, i: (b, h, 0, i)),
                  pl.BlockSpec((None, hp, nb, blk, dh), lambda b, h, i: (b, h, 0, 0, 0)),
                  pl.BlockSpec((None, hp, nb, dh, blk), lambda b, h, i: (b, h, 0, 0, 0)),
                  pl.BlockSpec((None, hp, nb, blk), lambda b, h, i: (b, h, 0, i))],
        out_specs=pl.BlockSpec((None, hp, dh, blk), lambda b, h, i: (b, h, 0, i)),
        out_shape=jax.ShapeDtypeStruct((bsz, H, dh, s), F32),
        scratch_shapes=[pltpu.VMEM((hp, nb, blk), F32)],
        compiler_params=_params(("parallel", "parallel", "arbitrary")),
        name="moba_attn",
    )(qT, k, vT, gate)


def _ret_log_decay():
    return np.log(1.0 - 2.0 ** (-5.0 - np.arange(RET_HEADS, dtype=np.float64)))


def _ret_kernel(x_ref, cos_ref, sin_ref, xi_ref, zeta_ref, dm_ref, o_ref, st_ref):
    C = RET_CHUNK

    @pl.when(pl.program_id(1) == 0)
    def _():
        st_ref[...] = jnp.zeros_like(st_ref)

    x = x_ref[...]
    q = x[:, 0:RET_W]
    k = x[:, RET_W:2 * RET_W]
    v = x[:, 2 * RET_W:3 * RET_W]
    g = x[:, 3 * RET_W:4 * RET_W]
    cos = cos_ref[...]
    sin = sin_ref[...]
    lane = lax.broadcasted_iota(jnp.int32, (C, RET_W), 1)
    first = (lane & (HEAD_DIM // 2)) == 0

    def rot(t):
        swapped = jnp.where(first, pltpu.roll(t, RET_W - HEAD_DIM // 2, 1), pltpu.roll(t, HEAD_DIM // 2, 1))
        return t * cos + swapped * sin

    qr = rot(q)
    kr = rot(k) * (HEAD_DIM ** -0.5)
    qb = qr.astype(BF16)
    kb = kr.astype(BF16)
    vb = v.astype(BF16)
    qx = (qr * xi_ref[...]).astype(BF16)
    kz = kr * zeta_ref[...]
    decay_c = np.exp(C * _ret_log_decay())
    for h in range(RET_HEADS):
        sl = slice(h * HEAD_DIM, (h + 1) * HEAD_DIM)
        sc = lax.dot_general(qb[:, sl], kb[:, sl], (((1,), (1,)), ((), ())),
                             preferred_element_type=F32) * dm_ref[h]
        inner = jnp.dot(sc.astype(BF16), vb[:, sl], preferred_element_type=F32)
        r_prev = st_ref[h]
        cross = jnp.dot(qx[:, sl], r_prev.astype(BF16), preferred_element_type=F32)
        kv = jnp.dot(kz[:, sl].T.astype(BF16), vb[:, sl], preferred_element_type=F32)
        st_ref[h] = float(decay_c[h]) * r_prev + kv
        o = inner + cross
        o = o * _rms_scale(o, -1)
        gh = g[:, sl]
        o_ref[:, sl] = gh * jax.nn.sigmoid(gh) * o


def _ret_tables(s):
    C = RET_CHUNK
    half = HEAD_DIM // 2
    pos = jnp.arange(s, dtype=F32)
    inv = RET_ANGLE_BASE ** (-jnp.linspace(0.0, 1.0, half, dtype=F32))
    ang = pos[:, None] * inv[None, :]
    cos, sin = jnp.cos(ang), jnp.sin(ang)
    cos_t = jnp.tile(cos, (1, 2 * RET_HEADS))
    sin_t = jnp.tile(jnp.concatenate([-sin, sin], axis=1), (1, RET_HEADS))
    lg = _ret_log_decay()
    i = np.arange(C, dtype=np.float64)
    xi = np.repeat(np.exp((i + 1.0)[:, None] * lg[None, :]), HEAD_DIM, axis=1)
    zeta = np.repeat(np.exp((C - 1.0 - i)[:, None] * lg[None, :]), HEAD_DIM, axis=1)
    diff = i[:, None] - i[None, :]
    dm = np.where(diff >= 0, np.exp(np.maximum(diff, 0.0)[None] * lg[:, None, None]), 0.0)
    return cos_t, sin_t, jnp.asarray(xi, F32), jnp.asarray(zeta, F32), jnp.asarray(dm, F32)


def _ret_mixer(rt, bsz, s, tables):
    C = RET_CHUNK
    nc = s // C
    cos_t, sin_t, xi, zeta, dm = tables
    return pl.pallas_call(
        _ret_kernel,
        grid=(bsz, nc),
        in_specs=[pl.BlockSpec((C, 4 * RET_W), lambda b, j: (b * nc + j, 0)),
                  pl.BlockSpec((C, RET_W), lambda b, j: (j, 0)),
                  pl.BlockSpec((C, RET_W), lambda b, j: (j, 0)),
                  pl.BlockSpec((C, RET_W), lambda b, j: (0, 0)),
                  pl.BlockSpec((C, RET_W), lambda b, j: (0, 0)),
                  pl.BlockSpec((RET_HEADS, C, C), lambda b, j: (0, 0, 0))],
        out_specs=pl.BlockSpec((C, RET_W), lambda b, j: (b * nc + j, 0)),
        out_shape=jax.ShapeDtypeStruct((bsz * s, RET_W), F32),
        scratch_shapes=[pltpu.VMEM((RET_HEADS, HEAD_DIM, HEAD_DIM), F32)],
        compiler_params=_params(("parallel", "arbitrary")),
        name="ret_mixer",
    )(rt, cos_t, sin_t, xi, zeta, dm)


def _out_proj_kernel(x_ref, ya_ref, ot_ref, yc_ref, g_ref, w_ref, o_ref):
    ts = x_ref.shape[0]
    ot = ot_ref[...]
    yb = (ot * _rms_scale(ot, 1)).reshape(MOBA_W, ts).T
    g = g_ref[...]
    a0, a1 = S5_CH, S5_CH + MOBA_W
    acc = x_ref[...]
    acc = acc + jnp.dot((ya_ref[...] * g[:, 0:a0]).astype(BF16), w_ref[0:a0, :], preferred_element_type=F32)
    acc = acc + jnp.dot((yb * g[:, a0:a1]).astype(BF16), w_ref[a0:a1, :], preferred_element_type=F32)
    acc = acc + jnp.dot((yc_ref[...] * g[:, a1:MIX_W]).astype(BF16), w_ref[a1:MIX_W, :],
                        preferred_element_type=F32)
    o_ref[...] = acc


def _out_proj(x2, ya, ot, yc, g, w, bsz, s):
    ts = PROJ_ROWS
    nt = s // ts
    return pl.pallas_call(
        _out_proj_kernel,
        grid=(bsz, nt),
        in_specs=[pl.BlockSpec((ts, D_MODEL), lambda b, j: (b * nt + j, 0)),
                  pl.BlockSpec((ts, S5_CH), lambda b, j: (b * nt + j, 0)),
                  pl.BlockSpec((None, MOBA_HEADS, HEAD_DIM, ts), lambda b, j: (b, 0, 0, j)),
                  pl.BlockSpec((ts, RET_W), lambda b, j: (b * nt + j, 0)),
                  pl.BlockSpec((1, MIX_W), lambda b, j: (0, 0)),
                  pl.BlockSpec((MIX_W, D_MODEL), lambda b, j: (0, 0))],
        out_specs=pl.BlockSpec((ts, D_MODEL), lambda b, j: (b * nt + j, 0)),
        out_shape=jax.ShapeDtypeStruct((bsz * s, D_MODEL), F32),
        compiler_params=_params(("parallel", "parallel")),
        name="out_proj",
    )(x2, ya, ot, yc, g, w)


def _top16(s, rank, payload=None):
    vals, pays = [], []
    for _ in range(PEER_TOPK):
        m = jnp.max(s, axis=0, keepdims=True)
        pos = jnp.min(jnp.where(s == m, rank, jnp.int32(2 ** 30)), axis=0, keepdims=True)
        hit = rank == pos
        vals.append(m)
        pays.append(pos if payload is None else jnp.sum(jnp.where(hit, payload, 0), axis=0, keepdims=True))
        s = jnp.where(hit, NEG_INF, s)
    return jnp.concatenate(vals, axis=0), jnp.concatenate(pays, axis=0)


def _pair_candidates(v0, i0, v1, i1):
    K, n = PEER_TOPK, v0.shape[1]
    j = lax.broadcasted_iota(jnp.int32, (8, n), 0)
    vals, flat, eid = [], [], []
    for a in range(4):
        vals.append(v0[a:a + 1] + v1[0:8])
        flat.append(j + a * K)
        eid.append(i0[a:a + 1] * PEER_KEYS + i1[0:8])
    vals.append(v0[0:1] + v1[8:16])
    flat.append(j + 8)
    eid.append(i0[0:1] * PEER_KEYS + i1[8:16])
    vals.append(v0[8:16] + v1[0:1])
    flat.append((j + 8) * K)
    eid.append(i0[8:16] * PEER_KEYS + i1[0:1])
    for b in range(3):
        vals.append(jnp.where(j >= 4, v0[0:8] + v1[b:b + 1], NEG_INF))
        flat.append(jnp.where(j >= 4, j * K + b, K * K + j * K + b))
        eid.append(i0[0:8] * PEER_KEYS + i1[b:b + 1])
    return jnp.concatenate(vals, axis=0), jnp.concatenate(flat, axis=0), jnp.concatenate(eid, axis=0)


def _peer_route_kernel(x_ref, g_ref, wqT_ref, sk_ref, h2_ref, idx_ref, gate_ref, q_scr):
    tb = x_ref.shape[0]
    x = x_ref[...]
    h2 = x * _rms_scale(x, -1) * g_ref[...]
    h2_ref[...] = h2
    q_scr[...] = lax.dot_general(wqT_ref[...], h2.astype(BF16), (((1,), (1,)), ((), ())),
                                 preferred_element_type=F32).astype(BF16)
    half = PEER_QDIM // 2
    io_keys = lax.broadcasted_iota(jnp.int32, (PEER_KEYS, tb), 0)

    def head(h, carry):
        sub = []
        for c in range(2):
            grp = 2 * h + c
            qg = q_scr[pl.ds(pl.multiple_of(grp * half, half), half), :]
            sc = jnp.dot(sk_ref[grp], qg, preferred_element_type=F32)
            sub.append(_top16(sc, io_keys))
        (v0, i0), (v1, i1) = sub
        fs, eidx = _top16(*_pair_candidates(v0, i0, v1, i1))
        e = jnp.exp(fs - fs[0:1, :])
        gates = e / jnp.sum(e, axis=0, keepdims=True)
        r0 = pl.multiple_of(h * PEER_TOPK, PEER_TOPK)
        idx_ref[pl.ds(r0, PEER_TOPK), :] = eidx
        gate_ref[pl.ds(r0, PEER_TOPK), :] = gates
        return carry

    lax.fori_loop(0, PEER_HEADS, head, 0)


def _peer_route(x2, g, wqT, sk):
    t = x2.shape[0]
    tb = PEER_ROUTE_TOKENS
    ngrp = 2 * PEER_HEADS
    return pl.pallas_call(
        _peer_route_kernel,
        grid=(t // tb,),
        in_specs=[pl.BlockSpec((tb, D_MODEL), lambda i: (i, 0)),
                  pl.BlockSpec((1, D_MODEL), lambda i: (0, 0)),
                  pl.BlockSpec((PEER_HEADS * PEER_QDIM, D_MODEL), lambda i: (0, 0)),
                  pl.BlockSpec((ngrp, PEER_KEYS, PEER_QDIM // 2), lambda i: (0, 0, 0))],
        out_specs=[pl.BlockSpec((tb, D_MODEL), lambda i: (i, 0)),
                   pl.BlockSpec((PEER_SEL, tb), lambda i: (0, i)),
                   pl.BlockSpec((PEER_SEL, tb), lambda i: (0, i))],
        out_shape=[jax.ShapeDtypeStruct((t, D_MODEL), F32),
                   jax.ShapeDtypeStruct((PEER_SEL, t), jnp.int32),
                   jax.ShapeDtypeStruct((PEER_SEL, t), F32)],
        scratch_shapes=[pltpu.VMEM((PEER_HEADS * PEER_QDIM, tb), BF16)],
        compiler_params=_params(("parallel",)),
        name="peer_route",
    )(x2, g, wqT, sk)


def _pack_table(tab):
    bits = lax.bitcast_convert_type(tab.astype(BF16), jnp.uint16).astype(jnp.uint32)
    half = D_MODEL // 2
    words = bits[:, :half] | (bits[:, half:] << 16)
    return lax.bitcast_convert_type(words, jnp.int32).reshape(tab.shape[0] * TABLE_ROWS_PER_EXPERT, LANES)


def _table_spec(tab):
    return pl.BlockSpec(tab.shape, lambda i: (0, 0), pipeline_mode=pl.Buffered(1))


def _gather_rows(idx_ref, t, tab_ref, tile_ref):
    R = TABLE_ROWS_PER_EXPERT
    for k in range(PEER_SEL):
        row = pl.multiple_of(idx_ref[t, k], R)
        tile_ref[pl.ds(k, R, stride=TILE_STRIDE), :] = tab_ref[pl.ds(row, R), :]


def _tile_chunk(tile_ref, r):
    return pltpu.bitcast(tile_ref[r * TILE_STRIDE:r * TILE_STRIDE + PEER_SEL, :], BF16)


def _pipelined_tokens(tb, idx_ref, tab_ref, tiles, compute, finish, init):
    tile_a, tile_b = tiles
    _gather_rows(idx_ref, 0, tab_ref, tile_a)
    _gather_rows(idx_ref, 1, tab_ref, tile_b)

    def pair(i, pending):
        t0 = 2 * i
        finish(jnp.maximum(t0 - 1, 0), pending)
        va = compute(t0, tile_a)
        _gather_rows(idx_ref, jnp.minimum(t0 + 2, tb - 1), tab_ref, tile_a)
        finish(t0, va)
        vb = compute(t0 + 1, tile_b)
        _gather_rows(idx_ref, jnp.minimum(t0 + 3, tb - 1), tab_ref, tile_b)
        return vb

    last = lax.fori_loop(0, tb // 2, pair, init)
    finish(tb - 1, last)


def _hi_lo_rows(v, lane_parity):
    n = v.shape[1]
    vb = jnp.broadcast_to(v, (8, n))
    hi = vb.astype(BF16).astype(F32)
    srow = lax.broadcasted_iota(jnp.int32, (8, n), 0)
    part = jnp.where(srow < 2, hi, vb - hi)
    part = jnp.where(srow < 4, part, 0.0)
    return jnp.where((srow & 1) == lane_parity, part, 0.0)


def _peer_act_kernel(idx_ref, x_ref, gate_ref, tab_ref, c_ref, tile_a, tile_b):
    tb = x_ref.shape[0]
    R = TABLE_ROWS_PER_EXPERT
    half = D_MODEL // 2
    lane = lax.broadcasted_iota(jnp.int32, (PEER_SEL, tb), 1)
    c_ref[...] = jnp.zeros_like(c_ref)

    def compute(t, tile_ref):
        x = x_ref[pl.ds(t, 1), :]
        acc = jnp.zeros((PEER_SEL, LANES), F32)
        for r in range(R):
            w = tile_ref[r * TILE_STRIDE:r * TILE_STRIDE + PEER_SEL, :]
            lo = lax.bitcast_convert_type(w << 16, F32)
            hi = lax.bitcast_convert_type(w & jnp.int32(-65536), F32)
            acc = acc + lo * x[:, r * LANES:(r + 1) * LANES]
            acc = acc + hi * x[:, half + r * LANES:half + (r + 1) * LANES]
        return jnp.sum(acc, axis=1, keepdims=True)

    def finish(t, col):
        c_ref[...] = jnp.where(lane == t, col, c_ref[...])

    _pipelined_tokens(tb, idx_ref, tab_ref, (tile_a, tile_b), compute, finish,
                      jnp.zeros((PEER_SEL, 1), F32))
    c_ref[...] = gate_ref[...] * _gelu_tanh(c_ref[...])


def _tile_scratch():
    shape = (TABLE_ROWS_PER_EXPERT * TILE_STRIDE, LANES)
    return [pltpu.VMEM(shape, jnp.int32), pltpu.VMEM(shape, jnp.int32)]


def _peer_act(idx, h2, gate_t, tab):
    t = h2.shape[0]
    tb = PEER_GATHER_TOKENS
    return pl.pallas_call(
        _peer_act_kernel,
        grid=(t // tb,),
        in_specs=[pl.BlockSpec((tb, PEER_SEL), lambda i: (i, 0), memory_space=pltpu.SMEM),
                  pl.BlockSpec((tb, D_MODEL), lambda i: (i, 0)),
                  pl.BlockSpec((PEER_SEL, tb), lambda i: (0, i)),
                  _table_spec(tab)],
        out_specs=pl.BlockSpec((PEER_SEL, tb), lambda i: (0, i)),
        out_shape=jax.ShapeDtypeStruct((PEER_SEL, t), F32),
        scratch_shapes=_tile_scratch(),
        compiler_params=_params(("arbitrary",)),
        name="peer_act",
    )(idx, h2, gate_t, tab)


def _peer_out_kernel(idx_ref, c_ref, tab_ref, o_ref, tile_a, tile_b):
    tb = o_ref.shape[0]
    R = TABLE_ROWS_PER_EXPERT
    half = D_MODEL // 2
    lane_parity = lax.broadcasted_iota(jnp.int32, (8, 2 * PEER_SEL), 1) & 1

    def compute(t, tile_ref):
        rows = _hi_lo_rows(c_ref[t], lane_parity).astype(BF16)
        res = [jnp.dot(rows, _tile_chunk(tile_ref, r), preferred_element_type=F32) for r in range(R)]
        first = jnp.concatenate([v[0:1] + v[2:3] for v in res], axis=1)
        second = jnp.concatenate([v[1:2] + v[3:4] for v in res], axis=1)
        return jnp.concatenate([first, second], axis=1)

    def finish(t, row):
        o_ref[t] = row

    _pipelined_tokens(tb, idx_ref, tab_ref, (tile_a, tile_b), compute, finish,
                      jnp.zeros((1, D_MODEL), F32))


def _peer_out(idx, c_i, tab):
    t = idx.shape[0]
    tb = PEER_GATHER_TOKENS
    return pl.pallas_call(
        _peer_out_kernel,
        grid=(t // tb,),
        in_specs=[pl.BlockSpec((tb, PEER_SEL), lambda i: (i, 0), memory_space=pltpu.SMEM),
                  pl.BlockSpec((tb, 1, 2 * PEER_SEL), lambda i: (i, 0, 0)),
                  _table_spec(tab)],
        out_specs=pl.BlockSpec((tb, 1, D_MODEL), lambda i: (i, 0, 0)),
        out_shape=jax.ShapeDtypeStruct((t, 1, D_MODEL), F32),
        scratch_shapes=_tile_scratch(),
        compiler_params=_params(("arbitrary",)),
        name="peer_out",
    )(idx, c_i, tab)


def _mixer_layer(x2, bsz, s, p, rope, ret_tables):
    ua, mb, rt = _in_proj(x2, p['norm1_g'], p['w_in'], p['w_in_qk_lo'])
    ya = _s5_mixer(ua, bsz, s, p['s5_bcat'], p['s5_ccat'], p['s5_tabs'], p['s5_d'], p['s5_w_glu'])
    qT, k, vT, gate = _moba_prep(mb, bsz, s, p['moba_q_g'], p['moba_k_g'], *rope)
    ot = _moba_attn(qT, k, vT, gate)
    yc = _ret_mixer(rt, bsz, s, ret_tables)
    return _out_proj(x2, ya, ot, yc, p['mix_out_g'], p['w_out'], bsz, s)


def _peer_layer(x2, p):
    h2, idx_t, gate_t = _peer_route(x2, p['norm2_g'], p['peer_wqT'], p['peer_sk'])
    idx = idx_t.T * TABLE_ROWS_PER_EXPERT
    c_t = _peer_act(idx, h2, gate_t, p['peer_u'])
    c_i = jnp.repeat(c_t.T, 2, axis=1)[:, None, :]
    return x2 + _peer_out(idx, c_i, p['peer_v']).reshape(x2.shape)


def kernel(x, norm1_g, w_in, s5_lam_re, s5_lam_im, s5_log_dt, s5_b_re, s5_b_im, s5_c_re, s5_c_im,
           s5_d, s5_w_glu, moba_q_g, moba_k_g, mix_out_g, w_out, norm2_g, peer_w_q, peer_sub_keys,
           peer_u, peer_v):
    bsz, s, d = x.shape
    depth = w_in.shape[0]
    pos = jnp.arange(s, dtype=F32)
    inv = ROPE_THETA ** (-jnp.arange(0, ROPE_DIMS, 2, dtype=F32) / ROPE_DIMS)
    ang = (pos[:, None] * inv[None, :]).T
    rope = (jnp.cos(ang), jnp.sin(ang))
    ret_tables = _ret_tables(s)
    x2 = x.reshape(bsz * s, d)
    for l in range(depth):
        bcat, ccat, tabs = _s5_operands(s5_lam_re[l], s5_lam_im[l], s5_log_dt[l], s5_b_re[l],
                                        s5_b_im[l], s5_c_re[l], s5_c_im[l])
        w_in_hi = w_in[l].astype(BF16)
        qk_cols = slice(S5_CH, S5_CH + 2 * MOBA_W)
        p = {
            'norm1_g': norm1_g[l][None, :],
            'w_in': w_in_hi,
            'w_in_qk_lo': (w_in[l][:, qk_cols] - w_in_hi[:, qk_cols].astype(F32)).astype(BF16),
            's5_bcat': bcat, 's5_ccat': ccat, 's5_tabs': tabs,
            's5_d': s5_d[l][None, :],
            's5_w_glu': s5_w_glu[l].astype(BF16),
            'moba_q_g': moba_q_g[l][:, None],
            'moba_k_g': moba_k_g[l][:, None],
            'mix_out_g': mix_out_g[l][None, :],
            'w_out': w_out[l].astype(BF16),
            'norm2_g': norm2_g[l][None, :],
            'peer_wqT': peer_w_q[l].T.astype(BF16),
            'peer_sk': peer_sub_keys[l].reshape(2 * PEER_HEADS, PEER_KEYS, PEER_QDIM // 2).astype(BF16),
            'peer_u': _pack_table(peer_u[l]),
            'peer_v': _pack_table(peer_v[l]),
        }
        x2 = _mixer_layer(x2, bsz, s, p, rope, ret_tables)
        x2 = _peer_layer(x2, p)
    return x2.reshape(bsz, s, d)
```

```python
import functools
import math

import numpy as np
import jax
import jax.numpy as jnp
from jax import lax
from jax.experimental import pallas as pl
from jax.experimental.pallas import tpu as pltpu

D_MODEL = 1024
HEAD_DIM = 64
S5_CH = 256
S5_GROUP = 16
S5_NGROUPS = 16
S5_STATE = 64
S5_W = S5_NGROUPS * S5_STATE
MOBA_HEADS = 8
MOBA_W = 512
MOBA_BLOCK = 256
MOBA_TOPK = 3
MOBA_GROUP = 4
MOBA_HEADS_PER_STEP = 4
ROPE_THETA = 500000.0
ROPE_DIMS = 16
RET_HEADS = 4
RET_W = 256
RET_ANGLE_BASE = 10000.0
MIX_W = S5_CH + MOBA_W + RET_W
IN_W = S5_CH + 3 * MOBA_W + 4 * RET_W
PEER_KEYS = 128
PEER_HEADS = 8
PEER_TOPK = 16
PEER_QDIM = 128
PEER_SEL = PEER_HEADS * PEER_TOPK
EPS = 1e-6

LANES = 128
TABLE_ROWS_PER_EXPERT = D_MODEL // (2 * LANES)
TILE_STRIDE = 136
VMEM_LIMIT = 56 * 1024 * 1024

S5_CHUNK = 128
RET_CHUNK = 256
PROJ_ROWS = 256
PEER_ROUTE_TOKENS = 256
PEER_GATHER_TOKENS = 128

BF16 = jnp.bfloat16
F32 = jnp.float32
NEG_INF = float("-inf")


def _params(sem):
    return pltpu.CompilerParams(dimension_semantics=sem, vmem_limit_bytes=VMEM_LIMIT)


def _gelu_tanh(y):
    return 0.5 * y * (1.0 + jnp.tanh(0.7978845608028654 * (y + 0.044715 * (y * y * y))))


def _rms_scale(x, axis):
    return lax.rsqrt(jnp.mean(x * x, axis=axis, keepdims=True) + EPS)


def _split(a):
    hi = a.astype(BF16)
    return hi, (a - hi.astype(F32)).astype(BF16)


def _dot3(a, b):
    a_hi, a_lo = _split(a)
    b_hi, b_lo = _split(b)
    out = jnp.dot(a_hi, b_hi, preferred_element_type=F32)
    out = out + jnp.dot(a_hi, b_lo, preferred_element_type=F32)
    return out + jnp.dot(a_lo, b_hi, preferred_element_type=F32)


def _in_proj_kernel(x_ref, g_ref, w_ref, wlo_ref, ua_ref, mb_ref, rt_ref):
    x = x_ref[...]
    hf = x * _rms_scale(x, -1) * g_ref[...]
    h = hf.astype(BF16)
    ua_ref[...] = jnp.dot(h, w_ref[:, 0:S5_CH], preferred_element_type=F32)
    h_lo = (hf - h.astype(F32)).astype(BF16)
    q0, q1 = S5_CH, S5_CH + 2 * MOBA_W
    qk = jnp.dot(h, w_ref[:, q0:q1], preferred_element_type=F32)
    qk = qk + jnp.dot(h, wlo_ref[...], preferred_element_type=F32)
    qk = qk + jnp.dot(h_lo, w_ref[:, q0:q1], preferred_element_type=F32)
    mb_ref[:, 0:2 * MOBA_W] = qk
    mb_ref[:, 2 * MOBA_W:3 * MOBA_W] = jnp.dot(h, w_ref[:, q1:q1 + MOBA_W], preferred_element_type=F32)
    rt_ref[...] = jnp.dot(h, w_ref[:, S5_CH + 3 * MOBA_W:IN_W], preferred_element_type=F32)


def _in_proj(x2, g, w, wlo):
    t = x2.shape[0]
    tm = PROJ_ROWS
    return pl.pallas_call(
        _in_proj_kernel,
        grid=(t // tm,),
        in_specs=[pl.BlockSpec((tm, D_MODEL), lambda i: (i, 0)),
                  pl.BlockSpec((1, D_MODEL), lambda i: (0, 0)),
                  pl.BlockSpec((D_MODEL, IN_W), lambda i: (0, 0)),
                  pl.BlockSpec((D_MODEL, 2 * MOBA_W), lambda i: (0, 0))],
        out_specs=[pl.BlockSpec((tm, S5_CH), lambda i: (i, 0)),
                   pl.BlockSpec((tm, 3 * MOBA_W), lambda i: (i, 0)),
                   pl.BlockSpec((tm, 4 * RET_W), lambda i: (i, 0))],
        out_shape=[jax.ShapeDtypeStruct((t, S5_CH), F32),
                   jax.ShapeDtypeStruct((t, 3 * MOBA_W), F32),
                   jax.ShapeDtypeStruct((t, 4 * RET_W), F32)],
        compiler_params=_params(("parallel",)),
        name="in_proj",
    )(x2, g, w, wlo)


def _s5_kernel(u_ref, bcat_ref, ccat_ref, tab_ref, d_ref, wglu_ref, o_ref, st_ref):
    L = S5_CHUNK

    @pl.when(pl.program_id(1) == 0)
    def _():
        st_ref[...] = jnp.zeros_like(st_ref)

    u = u_ref[...]
    bu = jnp.dot(u.astype(BF16), bcat_ref[...], preferred_element_type=F32)
    bur, bui = bu[:, :S5_W], bu[:, S5_W:]
    air, aii = tab_ref[0], tab_ref[1]
    ktr = bur * air - bui * aii
    kti = bur * aii + bui * air
    row = lax.broadcasted_iota(jnp.int32, (L, L), 0)
    col = lax.broadcasted_iota(jnp.int32, (L, L), 1)
    tri = jnp.where(row >= col, 1.0, 0.0).astype(BF16)
    cr = jnp.dot(tri, ktr.astype(BF16), preferred_element_type=F32)
    ci = jnp.dot(tri, kti.astype(BF16), preferred_element_type=F32)
    apr, api = tab_ref[2], tab_ref[3]
    acr, aci = tab_ref[4], tab_ref[5]
    pr = st_ref[0:1, :]
    pi = st_ref[1:2, :]
    sr = apr * cr - api * ci + acr * pr - aci * pi
    si = apr * ci + api * cr + acr * pi + aci * pr
    st_ref[0:1, :] = sr[L - 1:L, :]
    st_ref[1:2, :] = si[L - 1:L, :]
    y = (jnp.dot(sr.astype(BF16), ccat_ref[0:S5_W, :], preferred_element_type=F32)
         + jnp.dot(si.astype(BF16), ccat_ref[S5_W:2 * S5_W, :], preferred_element_type=F32))
    y = _gelu_tanh(y + u * d_ref[...])
    z = jnp.dot(y.astype(BF16), wglu_ref[...], preferred_element_type=F32)
    y = y * jax.nn.sigmoid(z)
    o_ref[...] = y * _rms_scale(y, -1)


def _s5_operands(lam_re, lam_im, log_dt, b_re, b_im, c_re, c_im):
    dt = jnp.exp(log_dt)[:, None]
    mag = jnp.exp(lam_re * dt)
    ar = mag * jnp.cos(lam_im * dt)
    ai = mag * jnp.sin(lam_im * dt)
    den = lam_re * lam_re + lam_im * lam_im
    fr = ((ar - 1.0) * lam_re + ai * lam_im) / den
    fi = (ai * lam_re - (ar - 1.0) * lam_im) / den
    bbr = fr[..., None] * b_re - fi[..., None] * b_im
    bbi = fr[..., None] * b_im + fi[..., None] * b_re
    eye = jnp.eye(S5_NGROUPS, dtype=F32)
    blk = lambda m: jnp.einsum('gph,gk->ghkp', m, eye).reshape(S5_CH, S5_W)
    bcat = jnp.concatenate([blk(bbr), blk(bbi)], axis=1).astype(BF16)
    blc = lambda m: jnp.einsum('ghp,gk->gpkh', m, eye).reshape(S5_W, S5_CH)
    ccat = jnp.concatenate([blc(c_re), -blc(c_im)], axis=0).astype(BF16)

    def powers(n):
        lr = (lam_re * dt).reshape(1, S5_W)
        li = (lam_im * dt).reshape(1, S5_W)
        m = jnp.exp(lr * n[:, None])
        return m * jnp.cos(li * n[:, None]), m * jnp.sin(li * n[:, None])

    j = jnp.arange(S5_CHUNK, dtype=F32)
    tabs = jnp.stack([*powers(-j), *powers(j), *powers(j + 1.0)], axis=0)
    return bcat, ccat, tabs


def _s5_mixer(ua, bsz, s, bcat, ccat, tabs, d_skip, w_glu):
    L = S5_CHUNK
    nc = s // L
    return pl.pallas_call(
        _s5_kernel,
        grid=(bsz, nc),
        in_specs=[pl.BlockSpec((L, S5_CH), lambda b, j: (b * nc + j, 0)),
                  pl.BlockSpec((S5_CH, 2 * S5_W), lambda b, j: (0, 0)),
                  pl.BlockSpec((2 * S5_W, S5_CH), lambda b, j: (0, 0)),
                  pl.BlockSpec((6, L, S5_W), lambda b, j: (0, 0, 0)),
                  pl.BlockSpec((1, S5_CH), lambda b, j: (0, 0)),
                  pl.BlockSpec((S5_CH, S5_CH), lambda b, j: (0, 0))],
        out_specs=pl.BlockSpec((L, S5_CH), lambda b, j: (b * nc + j, 0)),
        out_shape=jax.ShapeDtypeStruct((bsz * s, S5_CH), F32),
        scratch_shapes=[pltpu.VMEM((2, S5_W), F32)],
        compiler_params=_params(("parallel", "arbitrary")),
        name="s5_mixer",
    )(ua, bcat, ccat, tabs, d_skip, w_glu)


def _moba_prep_kernel(qkv_ref, gq_ref, gk_ref, cos_ref, sin_ref, qT_ref, k_ref, vT_ref, gate_ref, km_ref):
    j = pl.program_id(1)

    @pl.when(j == 0)
    def _():
        km_ref[...] = jnp.zeros_like(km_ref)

    x = qkv_ref[...]
    c = cos_ref[...]
    s = sin_ref[...]
    half = ROPE_DIMS // 2

    def prep(xT, g):
        y = xT * _rms_scale(xT, 0) * g
        x1 = y[0:half]
        x2 = y[half:ROPE_DIMS]
        return jnp.concatenate([x1 * c - x2 * s, x2 * c + x1 * s, y[ROPE_DIMS:]], axis=0)

    qT_all = x[:, 0:MOBA_W].T
    kT_all = x[:, MOBA_W:2 * MOBA_W].T
    vT_all = x[:, 2 * MOBA_W:3 * MOBA_W].T
    gq = gq_ref[...]
    gk = gk_ref[...]
    for h in range(MOBA_HEADS):
        sl = slice(h * HEAD_DIM, (h + 1) * HEAD_DIM)
        q = prep(qT_all[sl], gq) * (HEAD_DIM ** -0.5)
        qT_ref[h] = q.astype(BF16)
        gate_ref[h] = _dot3(km_ref[h], q)
        k = prep(kT_all[sl], gk).T
        k_ref[h] = k.astype(BF16)
        km_ref[h, pl.ds(j, 1), :] = jnp.mean(k, axis=0, keepdims=True)
        vT_ref[h] = vT_all[sl].astype(BF16)


def _moba_prep(qkv, bsz, s, gq, gk, cosT, sinT):
    nb = s // MOBA_BLOCK
    H, dh, blk = MOBA_HEADS, HEAD_DIM, MOBA_BLOCK
    return pl.pallas_call(
        _moba_prep_kernel,
        grid=(bsz, nb),
        in_specs=[pl.BlockSpec((blk, 3 * MOBA_W), lambda b, j: (b * nb + j, 0)),
                  pl.BlockSpec((dh, 1), lambda b, j: (0, 0)),
                  pl.BlockSpec((dh, 1), lambda b, j: (0, 0)),
                  pl.BlockSpec((ROPE_DIMS // 2, blk), lambda b, j: (0, j)),
                  pl.BlockSpec((ROPE_DIMS // 2, blk), lambda b, j: (0, j))],
        out_specs=[pl.BlockSpec((None, H, dh, blk), lambda b, j: (b, 0, 0, j)),
                   pl.BlockSpec((None, H, None, blk, dh), lambda b, j: (b, 0, j, 0, 0)),
                   pl.BlockSpec((None, H, None, dh, blk), lambda b, j: (b, 0, j, 0, 0)),
                   pl.BlockSpec((None, H, nb, blk), lambda b, j: (b, 0, 0, j))],
        out_shape=[jax.ShapeDtypeStruct((bsz, H, dh, s), BF16),
                   jax.ShapeDtypeStruct((bsz, H, nb, blk, dh), BF16),
                   jax.ShapeDtypeStruct((bsz, H, nb, dh, blk), BF16),
                   jax.ShapeDtypeStruct((bsz, H, nb, s), F32)],
        scratch_shapes=[pltpu.VMEM((H, nb, dh), F32)],
        compiler_params=_params(("parallel", "arbitrary")),
        name="moba_prep",
    )(qkv, gq, gk, cosT, sinT)


def _moba_attn_kernel(qT_ref, k_ref, vT_ref, gate_ref, o_ref, msk_ref):
    i = pl.program_id(2)
    hp, nb = gate_ref.shape[0], gate_ref.shape[1]
    blk = MOBA_BLOCK
    row = lax.broadcasted_iota(jnp.int32, (nb, blk), 0)
    kpos = lax.broadcasted_iota(jnp.int32, (blk, blk), 0)
    qpos = lax.broadcasted_iota(jnp.int32, (blk, blk), 1)
    qTs, init = [], []
    for h in range(hp):
        qT = qT_ref[h]
        gm = jnp.where(row < i, gate_ref[h], NEG_INF)
        cnt = jnp.zeros((nb, blk), jnp.int32)
        for m in range(nb):
            gmm = gm[m:m + 1, :]
            beats = jnp.where(gmm > gm, 1, jnp.where(gmm == gm, jnp.where(row > m, 1, 0), 0))
            cnt = cnt + beats
        msk_ref[h] = jnp.where(row < i, jnp.where(cnt < MOBA_TOPK, 0.0, NEG_INF), NEG_INF)

        sT = jnp.dot(k_ref[h, i], qT, preferred_element_type=F32)
        sT = jnp.where(kpos <= qpos, sT, NEG_INF)
        m0 = jnp.max(sT, axis=0, keepdims=True)
        p = jnp.exp(sT - m0)
        l0 = jnp.sum(p, axis=0, keepdims=True)
        acc0 = jnp.dot(vT_ref[h, i], p.astype(BF16), preferred_element_type=F32)
        qTs.append(qT)
        init.append((m0, l0, acc0))

    U = MOBA_GROUP

    def body(g, carry):
        blocks = [g * U + u for u in range(U)]
        out = []
        for h in range(hp):
            m, l, acc = carry[h]
            sTs = [jnp.dot(k_ref[h, n], qTs[h], preferred_element_type=F32) + msk_ref[h, pl.ds(n, 1), :]
                   for n in blocks]
            m_new = m
            for sT in sTs:
                m_new = jnp.maximum(m_new, jnp.max(sT, axis=0, keepdims=True))
            alpha = jnp.exp(m - m_new)
            l = alpha * l
            acc = alpha * acc
            for n, sT in zip(blocks, sTs):
                p = jnp.exp(sT - m_new)
                l = l + jnp.sum(p, axis=0, keepdims=True)
                acc = acc + jnp.dot(vT_ref[h, n], p.astype(BF16), preferred_element_type=F32)
            out.append((m_new, l, acc))
        return tuple(out)

    groups = lax.shift_right_logical(i + (U - 1), U.bit_length() - 1)
    final = lax.fori_loop(0, groups, body, tuple(init))
    for h in range(hp):
        _, l, acc = final[h]
        o_ref[h] = acc / l


def _moba_attn(qT, k, vT, gate):
    bsz, H, dh, s = qT.shape
    nb = s // MOBA_BLOCK
    blk = MOBA_BLOCK
    hp = MOBA_HEADS_PER_STEP
    return pl.pallas_call(
        _moba_attn_kernel,
        grid=(bsz, H // hp, nb),
        in_specs=[pl.BlockSpec((None, hp, dh, blk), lambda b, h, i: (b, h, 0, i)),
                  pl.BlockSpec((None, hp, nb, blk, dh), lambda b, h, i: (b, h, 0, 0, 0)),
                  pl.BlockSpec((None, hp, nb, dh, blk), lambda b, h, i: (b, h, 0, 0, 0)),
                  pl.BlockSpec((None, hp, nb, blk), lambda b, h, i: (b, h, 0, i))],
        out_specs=pl.BlockSpec((None, hp, dh, blk), lambda b, h, i: (b, h, 0, i)),
        out_shape=jax.ShapeDtypeStruct((bsz, H, dh, s), F32),
        scratch_shapes=[pltpu.VMEM((hp, nb, blk), F32)],
        compiler_params=_params(("parallel", "parallel", "arbitrary")),
        name="moba_attn",
    )(qT, k, vT, gate)


def _ret_log_decay():
    return np.log(1.0 - 2.0 ** (-5.0 - np.arange(RET_HEADS, dtype=np.float64)))


def _ret_kernel(x_ref, cos_ref, sin_ref, xi_ref, zeta_ref, dm_ref, o_ref, st_ref):
    C = RET_CHUNK

    @pl.when(pl.program_id(1) == 0)
    def _():
        st_ref[...] = jnp.zeros_like(st_ref)

    x = x_ref[...]
    q = x[:, 0:RET_W]
    k = x[:, RET_W:2 * RET_W]
    v = x[:, 2 * RET_W:3 * RET_W]
    g = x[:, 3 * RET_W:4 * RET_W]
    cos = cos_ref[...]
    sin = sin_ref[...]
    lane = lax.broadcasted_iota(jnp.int32, (C, RET_W), 1)
    first = (lane & (HEAD_DIM // 2)) == 0

    def rot(t):
        swapped = jnp.where(first, pltpu.roll(t, RET_W - HEAD_DIM // 2, 1), pltpu.roll(t, HEAD_DIM // 2, 1))
        return t * cos + swapped * sin

    qr = rot(q)
    kr = rot(k) * (HEAD_DIM ** -0.5)
    qb = qr.astype(BF16)
    kb = kr.astype(BF16)
    vb = v.astype(BF16)
    qx = (qr * xi_ref[...]).astype(BF16)
    kz = kr * zeta_ref[...]
    decay_c = np.exp(C * _ret_log_decay())
    for h in range(RET_HEADS):
        sl = slice(h * HEAD_DIM, (h + 1) * HEAD_DIM)
        sc = lax.dot_general(qb[:, sl], kb[:, sl], (((1,), (1,)), ((), ())),
                             preferred_element_type=F32) * dm_ref[h]
        inner = jnp.dot(sc.astype(BF16), vb[:, sl], preferred_element_type=F32)
        r_prev = st_ref[h]
        cross = jnp.dot(qx[:, sl], r_prev.astype(BF16), preferred_element_type=F32)
        kv = jnp.dot(kz[:, sl].T.astype(BF16), vb[:, sl], preferred_element_type=F32)
        st_ref[h] = float(decay_c[h]) * r_prev + kv
        o = inner + cross
        o = o * _rms_scale(o, -1)
        gh = g[:, sl]
        o_ref[:, sl] = gh * jax.nn.sigmoid(gh) * o


def _ret_tables(s):
    C = RET_CHUNK
    half = HEAD_DIM // 2
    pos = jnp.arange(s, dtype=F32)
    inv = RET_ANGLE_BASE ** (-jnp.linspace(0.0, 1.0, half, dtype=F32))
    ang = pos[:, None] * inv[None, :]
    cos, sin = jnp.cos(ang), jnp.sin(ang)
    cos_t = jnp.tile(cos, (1, 2 * RET_HEADS))
    sin_t = jnp.tile(jnp.concatenate([-sin, sin], axis=1), (1, RET_HEADS))
    lg = _ret_log_decay()
    i = np.arange(C, dtype=np.float64)
    xi = np.repeat(np.exp((i + 1.0)[:, None] * lg[None, :]), HEAD_DIM, axis=1)
    zeta = np.repeat(np.exp((C - 1.0 - i)[:, None] * lg[None, :]), HEAD_DIM, axis=1)
    diff = i[:, None] - i[None, :]
    dm = np.where(diff >= 0, np.exp(np.maximum(diff, 0.0)[None] * lg[:, None, None]), 0.0)
    return cos_t, sin_t, jnp.asarray(xi, F32), jnp.asarray(zeta, F32), jnp.asarray(dm, F32)


def _ret_mixer(rt, bsz, s, tables):
    C = RET_CHUNK
    nc = s // C
    cos_t, sin_t, xi, zeta, dm = tables
    return pl.pallas_call(
        _ret_kernel,
        grid=(bsz, nc),
        in_specs=[pl.BlockSpec((C, 4 * RET_W), lambda b, j: (b * nc + j, 0)),
                  pl.BlockSpec((C, RET_W), lambda b, j: (j, 0)),
                  pl.BlockSpec((C, RET_W), lambda b, j: (j, 0)),
                  pl.BlockSpec((C, RET_W), lambda b, j: (0, 0)),
                  pl.BlockSpec((C, RET_W), lambda b, j: (0, 0)),
                  pl.BlockSpec((RET_HEADS, C, C), lambda b, j: (0, 0, 0))],
        out_specs=pl.BlockSpec((C, RET_W), lambda b, j: (b * nc + j, 0)),
        out_shape=jax.ShapeDtypeStruct((bsz * s, RET_W), F32),
        scratch_shapes=[pltpu.VMEM((RET_HEADS, HEAD_DIM, HEAD_DIM), F32)],
        compiler_params=_params(("parallel", "arbitrary")),
        name="ret_mixer",
    )(rt, cos_t, sin_t, xi, zeta, dm)


def _out_proj_kernel(x_ref, ya_ref, ot_ref, yc_ref, g_ref, w_ref, o_ref):
    ts = x_ref.shape[0]
    ot = ot_ref[...]
    yb = (ot * _rms_scale(ot, 1)).reshape(MOBA_W, ts).T
    g = g_ref[...]
    a0, a1 = S5_CH, S5_CH + MOBA_W
    acc = x_ref[...]
    acc = acc + jnp.dot((ya_ref[...] * g[:, 0:a0]).astype(BF16), w_ref[0:a0, :], preferred_element_type=F32)
    acc = acc + jnp.dot((yb * g[:, a0:a1]).astype(BF16), w_ref[a0:a1, :], preferred_element_type=F32)
    acc = acc + jnp.dot((yc_ref[...] * g[:, a1:MIX_W]).astype(BF16), w_ref[a1:MIX_W, :],
                        preferred_element_type=F32)
    o_ref[...] = acc


def _out_proj(x2, ya, ot, yc, g, w, bsz, s):
    ts = PROJ_ROWS
    nt = s // ts
    return pl.pallas_call(
        _out_proj_kernel,
        grid=(bsz, nt),
        in_specs=[pl.BlockSpec((ts, D_MODEL), lambda b, j: (b * nt + j, 0)),
                  pl.BlockSpec((ts, S5_CH), lambda b, j: (b * nt + j, 0)),
                  pl.BlockSpec((None, MOBA_HEADS, HEAD_DIM, ts), lambda b, j: (b, 0, 0, j)),
                  pl.BlockSpec((ts, RET_W), lambda b, j: (b * nt + j, 0)),
                  pl.BlockSpec((1, MIX_W), lambda b, j: (0, 0)),
                  pl.BlockSpec((MIX_W, D_MODEL), lambda b, j: (0, 0))],
        out_specs=pl.BlockSpec((ts, D_MODEL), lambda b, j: (b * nt + j, 0)),
        out_shape=jax.ShapeDtypeStruct((bsz * s, D_MODEL), F32),
        compiler_params=_params(("parallel", "parallel")),
        name="out_proj",
    )(x2, ya, ot, yc, g, w)


def _top16(s, rank, payload=None):
    vals, pays = [], []
    for _ in range(PEER_TOPK):
        m = jnp.max(s, axis=0, keepdims=True)
        pos = jnp.min(jnp.where(s == m, rank, jnp.int32(2 ** 30)), axis=0, keepdims=True)
        hit = rank == pos
        vals.append(m)
        pays.append(pos if payload is None else jnp.sum(jnp.where(hit, payload, 0), axis=0, keepdims=True))
        s = jnp.where(hit, NEG_INF, s)
    return jnp.concatenate(vals, axis=0), jnp.concatenate(pays, axis=0)


def _pair_candidates(v0, i0, v1, i1):
    K, n = PEER_TOPK, v0.shape[1]
    j = lax.broadcasted_iota(jnp.int32, (8, n), 0)
    vals, flat, eid = [], [], []
    for a in range(4):
        vals.append(v0[a:a + 1] + v1[0:8])
        flat.append(j + a * K)
        eid.append(i0[a:a + 1] * PEER_KEYS + i1[0:8])
    vals.append(v0[0:1] + v1[8:16])
    flat.append(j + 8)
    eid.append(i0[0:1] * PEER_KEYS + i1[8:16])
    vals.append(v0[8:16] + v1[0:1])
    flat.append((j + 8) * K)
    eid.append(i0[8:16] * PEER_KEYS + i1[0:1])
    for b in range(3):
        vals.append(jnp.where(j >= 4, v0[0:8] + v1[b:b + 1], NEG_INF))
        flat.append(jnp.where(j >= 4, j * K + b, K * K + j * K + b))
        eid.append(i0[0:8] * PEER_KEYS + i1[b:b + 1])
    return jnp.concatenate(vals, axis=0), jnp.concatenate(flat, axis=0), jnp.concatenate(eid, axis=0)


def _peer_route_kernel(x_ref, g_ref, wqT_ref, sk_ref, h2_ref, idx_ref, gate_ref, q_scr):
    tb = x_ref.shape[0]
    x = x_ref[...]
    h2 = x * _rms_scale(x, -1) * g_ref[...]
    h2_ref[...] = h2
    q_scr[...] = lax.dot_general(wqT_ref[...], h2.astype(BF16), (((1,), (1,)), ((), ())),
                                 preferred_element_type=F32).astype(BF16)
    half = PEER_QDIM // 2
    io_keys = lax.broadcasted_iota(jnp.int32, (PEER_KEYS, tb), 0)

    def head(h, carry):
        sub = []
        for c in range(2):
            grp = 2 * h + c
            qg = q_scr[pl.ds(pl.multiple_of(grp * half, half), half), :]
            sc = jnp.dot(sk_ref[grp], qg, preferred_element_type=F32)
            sub.append(_top16(sc, io_keys))
        (v0, i0), (v1, i1) = sub
        fs, eidx = _top16(*_pair_candidates(v0, i0, v1, i1))
        e = jnp.exp(fs - fs[0:1, :])
        gates = e / jnp.sum(e, axis=0, keepdims=True)
        r0 = pl.multiple_of(h * PEER_TOPK, PEER_TOPK)
        idx_ref[pl.ds(r0, PEER_TOPK), :] = eidx
        gate_ref[pl.ds(r0, PEER_TOPK), :] = gates
        return carry

    lax.fori_loop(0, PEER_HEADS, head, 0)


def _peer_route(x2, g, wqT, sk):
    t = x2.shape[0]
    tb = PEER_ROUTE_TOKENS
    ngrp = 2 * PEER_HEADS
    return pl.pallas_call(
        _peer_route_kernel,
        grid=(t // tb,),
        in_specs=[pl.BlockSpec((tb, D_MODEL), lambda i: (i, 0)),
                  pl.BlockSpec((1, D_MODEL), lambda i: (0, 0)),
                  pl.BlockSpec((PEER_HEADS * PEER_QDIM, D_MODEL), lambda i: (0, 0)),
                  pl.BlockSpec((ngrp, PEER_KEYS, PEER_QDIM // 2), lambda i: (0, 0, 0))],
        out_specs=[pl.BlockSpec((tb, D_MODEL), lambda i: (i, 0)),
                   pl.BlockSpec((PEER_SEL, tb), lambda i: (0, i)),
                   pl.BlockSpec((PEER_SEL, tb), lambda i: (0, i))],
        out_shape=[jax.ShapeDtypeStruct((t, D_MODEL), F32),
                   jax.ShapeDtypeStruct((PEER_SEL, t), jnp.int32),
                   jax.ShapeDtypeStruct((PEER_SEL, t), F32)],
        scratch_shapes=[pltpu.VMEM((PEER_HEADS * PEER_QDIM, tb), BF16)],
        compiler_params=_params(("parallel",)),
        name="peer_route",
    )(x2, g, wqT, sk)


def _pack_table(tab):
    bits = lax.bitcast_convert_type(tab.astype(BF16), jnp.uint16).astype(jnp.uint32)
    half = D_MODEL // 2
    words = bits[:, :half] | (bits[:, half:] << 16)
    return lax.bitcast_convert_type(words, jnp.int32).reshape(tab.shape[0] * TABLE_ROWS_PER_EXPERT, LANES)


def _table_spec(tab):
    return pl.BlockSpec(tab.shape, lambda i: (0, 0), pipeline_mode=pl.Buffered(1))


def _gather_rows(idx_ref, t, tab_ref, tile_ref):
    R = TABLE_ROWS_PER_EXPERT
    for k in range(PEER_SEL):
        row = pl.multiple_of(idx_ref[t, k], R)
        tile_ref[pl.ds(k, R, stride=TILE_STRIDE), :] = tab_ref[pl.ds(row, R), :]


def _tile_chunk(tile_ref, r):
    return pltpu.bitcast(tile_ref[r * TILE_STRIDE:r * TILE_STRIDE + PEER_SEL, :], BF16)


def _pipelined_tokens(tb, idx_ref, tab_ref, tiles, compute, finish, init):
    tile_a, tile_b = tiles
    _gather_rows(idx_ref, 0, tab_ref, tile_a)
    _gather_rows(idx_ref, 1, tab_ref, tile_b)

    def pair(i, pending):
        t0 = 2 * i
        finish(jnp.maximum(t0 - 1, 0), pending)
        va = compute(t0, tile_a)
        _gather_rows(idx_ref, jnp.minimum(t0 + 2, tb - 1), tab_ref, tile_a)
        finish(t0, va)
        vb = compute(t0 + 1, tile_b)
        _gather_rows(idx_ref, jnp.minimum(t0 + 3, tb - 1), tab_ref, tile_b)
        return vb

    last = lax.fori_loop(0, tb // 2, pair, init)
    finish(tb - 1, last)


def _hi_lo_rows(v, lane_parity):
    n = v.shape[1]
    vb = jnp.broadcast_to(v, (8, n))
    hi = vb.astype(BF16).astype(F32)
    srow = lax.broadcasted_iota(jnp.int32, (8, n), 0)
    part = jnp.where(srow < 2, hi, vb - hi)
    part = jnp.where(srow < 4, part, 0.0)
    return jnp.where((srow & 1) == lane_parity, part, 0.0)


def _peer_act_kernel(idx_ref, x_ref, gate_ref, tab_ref, c_ref, tile_a, tile_b):
    tb = x_ref.shape[0]
    R = TABLE_ROWS_PER_EXPERT
    half = D_MODEL // 2
    lane = lax.broadcasted_iota(jnp.int32, (PEER_SEL, tb), 1)
    c_ref[...] = jnp.zeros_like(c_ref)

    def compute(t, tile_ref):
        x = x_ref[pl.ds(t, 1), :]
        acc = jnp.zeros((PEER_SEL, LANES), F32)
        for r in range(R):
            w = tile_ref[r * TILE_STRIDE:r * TILE_STRIDE + PEER_SEL, :]
            lo = lax.bitcast_convert_type(w << 16, F32)
            hi = lax.bitcast_convert_type(w & jnp.int32(-65536), F32)
            acc = acc + lo * x[:, r * LANES:(r + 1) * LANES]
            acc = acc + hi * x[:, half + r * LANES:half + (r + 1) * LANES]
        return jnp.sum(acc, axis=1, keepdims=True)

    def finish(t, col):
        c_ref[...] = jnp.where(lane == t, col, c_ref[...])

    _pipelined_tokens(tb, idx_ref, tab_ref, (tile_a, tile_b), compute, finish,
                      jnp.zeros((PEER_SEL, 1), F32))
    c_ref[...] = gate_ref[...] * _gelu_tanh(c_ref[...])


def _tile_scratch():
    shape = (TABLE_ROWS_PER_EXPERT * TILE_STRIDE, LANES)
    return [pltpu.VMEM(shape, jnp.int32), pltpu.VMEM(shape, jnp.int32)]


def _peer_act(idx, h2, gate_t, tab):
    t = h2.shape[0]
    tb = PEER_GATHER_TOKENS
    return pl.pallas_call(
        _peer_act_kernel,
        grid=(t // tb,),
        in_specs=[pl.BlockSpec((tb, PEER_SEL), lambda i: (i, 0), memory_space=pltpu.SMEM),
                  pl.BlockSpec((tb, D_MODEL), lambda i: (i, 0)),
                  pl.BlockSpec((PEER_SEL, tb), lambda i: (0, i)),
                  _table_spec(tab)],
        out_specs=pl.BlockSpec((PEER_SEL, tb), lambda i: (0, i)),
        out_shape=jax.ShapeDtypeStruct((PEER_SEL, t), F32),
        scratch_shapes=_tile_scratch(),
        compiler_params=_params(("arbitrary",)),
        name="peer_act",
    )(idx, h2, gate_t, tab)


def _peer_out_kernel(idx_ref, c_ref, tab_ref, o_ref, tile_a, tile_b):
    tb = o_ref.shape[0]
    R = TABLE_ROWS_PER_EXPERT
    half = D_MODEL // 2
    lane_parity = lax.broadcasted_iota(jnp.int32, (8, 2 * PEER_SEL), 1) & 1

    def compute(t, tile_ref):
        rows = _hi_lo_rows(c_ref[t], lane_parity).astype(BF16)
        res = [jnp.dot(rows, _tile_chunk(tile_ref, r), preferred_element_type=F32) for r in range(R)]
        first = jnp.concatenate([v[0:1] + v[2:3] for v in res], axis=1)
        second = jnp.concatenate([v[1:2] + v[3:4] for v in res], axis=1)
        return jnp.concatenate([first, second], axis=1)

    def finish(t, row):
        o_ref[t] = row

    _pipelined_tokens(tb, idx_ref, tab_ref, (tile_a, tile_b), compute, finish,
                      jnp.zeros((1, D_MODEL), F32))


def _peer_out(idx, c_i, tab):
    t = idx.shape[0]
    tb = PEER_GATHER_TOKENS
    return pl.pallas_call(
        _peer_out_kernel,
        grid=(t // tb,),
        in_specs=[pl.BlockSpec((tb, PEER_SEL), lambda i: (i, 0), memory_space=pltpu.SMEM),
                  pl.BlockSpec((tb, 1, 2 * PEER_SEL), lambda i: (i, 0, 0)),
                  _table_spec(tab)],
        out_specs=pl.BlockSpec((tb, 1, D_MODEL), lambda i: (i, 0, 0)),
        out_shape=jax.ShapeDtypeStruct((t, 1, D_MODEL), F32),
        scratch_shapes=_tile_scratch(),
        compiler_params=_params(("arbitrary",)),
        name="peer_out",
    )(idx, c_i, tab)


def _mixer_layer(x2, bsz, s, p, rope, ret_tables):
    ua, mb, rt = _in_proj(x2, p['norm1_g'], p['w_in'], p['w_in_qk_lo'])
    ya = _s5_mixer(ua, bsz, s, p['s5_bcat'], p['s5_ccat'], p['s5_tabs'], p['s5_d'], p['s5_w_glu'])
    qT, k, vT, gate = _moba_prep(mb, bsz, s, p['moba_q_g'], p['moba_k_g'], *rope)
    ot = _moba_attn(qT, k, vT, gate)
    yc = _ret_mixer(rt, bsz, s, ret_tables)
    return _out_proj(x2, ya, ot, yc, p['mix_out_g'], p['w_out'], bsz, s)


def _peer_layer(x2, p):
    h2, idx_t, gate_t = _peer_route(x2, p['norm2_g'], p['peer_wqT'], p['peer_sk'])
    idx = idx_t.T * TABLE_ROWS_PER_EXPERT
    c_t = _peer_act(idx, h2, gate_t, p['peer_u'])
    c_i = jnp.repeat(c_t.T, 2, axis=1)[:, None, :]
    return x2 + _peer_out(idx, c_i, p['peer_v']).reshape(x2.shape)


def kernel(x, norm1_g, w_in, s5_lam_re, s5_lam_im, s5_log_dt, s5_b_re, s5_b_im, s5_c_re, s5_c_im,
           s5_d, s5_w_glu, moba_q_g, moba_k_g, mix_out_g, w_out, norm2_g, peer_w_q, peer_sub_keys,
           peer_u, peer_v):
    bsz, s, d = x.shape
    depth = w_in.shape[0]
    pos = jnp.arange(s, dtype=F32)
    inv = ROPE_THETA ** (-jnp.arange(0, ROPE_DIMS, 2, dtype=F32) / ROPE_DIMS)
    ang = (pos[:, None] * inv[None, :]).T
    rope = (jnp.cos(ang), jnp.sin(ang))
    ret_tables = _ret_tables(s)
    x2 = x.reshape(bsz * s, d)
    for l in range(depth):
        bcat, ccat, tabs = _s5_operands(s5_lam_re[l], s5_lam_im[l], s5_log_dt[l], s5_b_re[l],
                                        s5_b_im[l], s5_c_re[l], s5_c_im[l])
        w_in_hi = w_in[l].astype(BF16)
        qk_cols = slice(S5_CH, S5_CH + 2 * MOBA_W)
        p = {
            'norm1_g': norm1_g[l][None, :],
            'w_in': w_in_hi,
            'w_in_qk_lo': (w_in[l][:, qk_cols] - w_in_hi[:, qk_cols].astype(F32)).astype(BF16),
            's5_bcat': bcat, 's5_ccat': ccat, 's5_tabs': tabs,
            's5_d': s5_d[l][None, :],
            's5_w_glu': s5_w_glu[l].astype(BF16),
            'moba_q_g': moba_q_g[l][:, None],
            'moba_k_g': moba_k_g[l][:, None],
            'mix_out_g': mix_out_g[l][None, :],
            'w_out': w_out[l].astype(BF16),
            'norm2_g': norm2_g[l][None, :],
            'peer_wqT': peer_w_q[l].T.astype(BF16),
            'peer_sk': peer_sub_keys[l].reshape(2 * PEER_HEADS, PEER_KEYS, PEER_QDIM // 2).astype(BF16),
            'peer_u': _pack_table(peer_u[l]),
            'peer_v': _pack_table(peer_v[l]),
        }
        x2 = _mixer_layer(x2, bsz, s, p, rope, ret_tables)
        x2 = _peer_layer(x2, p)
    return x2.reshape(bsz, s, d)
```

```python
import functools
import math

import numpy as np
import jax
import jax.numpy as jnp
from jax import lax
from jax.experimental import pallas as pl
from jax.experimental.pallas import tpu as pltpu

D_MODEL = 1024
HEAD_DIM = 64
S5_CH = 256
S5_GROUP = 16
S5_NGROUPS = 16
S5_STATE = 64
S5_W = S5_NGROUPS * S5_STATE
MOBA_HEADS = 8
MOBA_W = 512
MOBA_BLOCK = 256
MOBA_TOPK = 3
MOBA_GROUP = 4
MOBA_HEADS_PER_STEP = 4
ROPE_THETA = 500000.0
ROPE_DIMS = 16
RET_HEADS = 4
RET_W = 256
RET_ANGLE_BASE = 10000.0
MIX_W = S5_CH + MOBA_W + RET_W
IN_W = S5_CH + 3 * MOBA_W + 4 * RET_W
PEER_KEYS = 128
PEER_HEADS = 8
PEER_TOPK = 16
PEER_QDIM = 128
PEER_SEL = PEER_HEADS * PEER_TOPK
EPS = 1e-6

LANES = 128
TABLE_ROWS_PER_EXPERT = D_MODEL // (2 * LANES)
TILE_STRIDE = 136
VMEM_LIMIT = 56 * 1024 * 1024

S5_CHUNK = 128
RET_CHUNK = 256
PROJ_ROWS = 256
PEER_ROUTE_TOKENS = 256
PEER_GATHER_TOKENS = 128

BF16 = jnp.bfloat16
F32 = jnp.float32
NEG_INF = float("-inf")


def _params(sem):
    return pltpu.CompilerParams(dimension_semantics=sem, vmem_limit_bytes=VMEM_LIMIT)


def _gelu_tanh(y):
    return 0.5 * y * (1.0 + jnp.tanh(0.7978845608028654 * (y + 0.044715 * (y * y * y))))


def _rms_scale(x, axis):
    return lax.rsqrt(jnp.mean(x * x, axis=axis, keepdims=True) + EPS)


def _split(a):
    hi = a.astype(BF16)
    return hi, (a - hi.astype(F32)).astype(BF16)


def _dot3(a, b):
    a_hi, a_lo = _split(a)
    b_hi, b_lo = _split(b)
    out = jnp.dot(a_hi, b_hi, preferred_element_type=F32)
    out = out + jnp.dot(a_hi, b_lo, preferred_element_type=F32)
    return out + jnp.dot(a_lo, b_hi, preferred_element_type=F32)


def _in_proj_kernel(x_ref, g_ref, w_ref, wlo_ref, ua_ref, mb_ref, rt_ref):
    x = x_ref[...]
    hf = x * _rms_scale(x, -1) * g_ref[...]
    h = hf.astype(BF16)
    ua_ref[...] = jnp.dot(h, w_ref[:, 0:S5_CH], preferred_element_type=F32)
    h_lo = (hf - h.astype(F32)).astype(BF16)
    q0, q1 = S5_CH, S5_CH + 2 * MOBA_W
    qk = jnp.dot(h, w_ref[:, q0:q1], preferred_element_type=F32)
    qk = qk + jnp.dot(h, wlo_ref[...], preferred_element_type=F32)
    qk = qk + jnp.dot(h_lo, w_ref[:, q0:q1], preferred_element_type=F32)
    mb_ref[:, 0:2 * MOBA_W] = qk
    mb_ref[:, 2 * MOBA_W:3 * MOBA_W] = jnp.dot(h, w_ref[:, q1:q1 + MOBA_W], preferred_element_type=F32)
    rt_ref[...] = jnp.dot(h, w_ref[:, S5_CH + 3 * MOBA_W:IN_W], preferred_element_type=F32)


def _in_proj(x2, g, w, wlo):
    t = x2.shape[0]
    tm = PROJ_ROWS
    return pl.pallas_call(
        _in_proj_kernel,
        grid=(t // tm,),
        in_specs=[pl.BlockSpec((tm, D_MODEL), lambda i: (i, 0)),
                  pl.BlockSpec((1, D_MODEL), lambda i: (0, 0)),
                  pl.BlockSpec((D_MODEL, IN_W), lambda i: (0, 0)),
                  pl.BlockSpec((D_MODEL, 2 * MOBA_W), lambda i: (0, 0))],
        out_specs=[pl.BlockSpec((tm, S5_CH), lambda i: (i, 0)),
                   pl.BlockSpec((tm, 3 * MOBA_W), lambda i: (i, 0)),
                   pl.BlockSpec((tm, 4 * RET_W), lambda i: (i, 0))],
        out_shape=[jax.ShapeDtypeStruct((t, S5_CH), F32),
                   jax.ShapeDtypeStruct((t, 3 * MOBA_W), F32),
                   jax.ShapeDtypeStruct((t, 4 * RET_W), F32)],
        compiler_params=_params(("parallel",)),
        name="in_proj",
    )(x2, g, w, wlo)


def _s5_kernel(u_ref, bcat_ref, ccat_ref, tab_ref, d_ref, wglu_ref, o_ref, st_ref):
    L = S5_CHUNK

    @pl.when(pl.program_id(1) == 0)
    def _():
        st_ref[...] = jnp.zeros_like(st_ref)

    u = u_ref[...]
    bu = jnp.dot(u.astype(BF16), bcat_ref[...], preferred_element_type=F32)
    bur, bui = bu[:, :S5_W], bu[:, S5_W:]
    air, aii = tab_ref[0], tab_ref[1]
    ktr = bur * air - bui * aii
    kti = bur * aii + bui * air
    row = lax.broadcasted_iota(jnp.int32, (L, L), 0)
    col = lax.broadcasted_iota(jnp.int32, (L, L), 1)
    tri = jnp.where(row >= col, 1.0, 0.0).astype(BF16)
    cr = jnp.dot(tri, ktr.astype(BF16), preferred_element_type=F32)
    ci = jnp.dot(tri, kti.astype(BF16), preferred_element_type=F32)
    apr, api = tab_ref[2], tab_ref[3]
    acr, aci = tab_ref[4], tab_ref[5]
    pr = st_ref[0:1, :]
    pi = st_ref[1:2, :]
    sr = apr * cr - api * ci + acr * pr - aci * pi
    si = apr * ci + api * cr + acr * pi + aci * pr
    st_ref[0:1, :] = sr[L - 1:L, :]
    st_ref[1:2, :] = si[L - 1:L, :]
    y = (jnp.dot(sr.astype(BF16), ccat_ref[0:S5_W, :], preferred_element_type=F32)
         + jnp.dot(si.astype(BF16), ccat_ref[S5_W:2 * S5_W, :], preferred_element_type=F32))
    y = _gelu_tanh(y + u * d_ref[...])
    z = jnp.dot(y.astype(BF16), wglu_ref[...], preferred_element_type=F32)
    y = y * jax.nn.sigmoid(z)
    o_ref[...] = y * _rms_scale(y, -1)


def _s5_operands(lam_re, lam_im, log_dt, b_re, b_im, c_re, c_im):
    dt = jnp.exp(log_dt)[:, None]
    mag = jnp.exp(lam_re * dt)
    ar = mag * jnp.cos(lam_im * dt)
    ai = mag * jnp.sin(lam_im * dt)
    den = lam_re * lam_re + lam_im * lam_im
    fr = ((ar - 1.0) * lam_re + ai * lam_im) / den
    fi = (ai * lam_re - (ar - 1.0) * lam_im) / den
    bbr = fr[..., None] * b_re - fi[..., None] * b_im
    bbi = fr[..., None] * b_im + fi[..., None] * b_re
    eye = jnp.eye(S5_NGROUPS, dtype=F32)
    blk = lambda m: jnp.einsum('gph,gk->ghkp', m, eye).reshape(S5_CH, S5_W)
    bcat = jnp.concatenate([blk(bbr), blk(bbi)], axis=1).astype(BF16)
    blc = lambda m: jnp.einsum('ghp,gk->gpkh', m, eye).reshape(S5_W, S5_CH)
    ccat = jnp.concatenate([blc(c_re), -blc(c_im)], axis=0).astype(BF16)

    def powers(n):
        lr = (lam_re * dt).reshape(1, S5_W)
        li = (lam_im * dt).reshape(1, S5_W)
        m = jnp.exp(lr * n[:, None])
        return m * jnp.cos(li * n[:, None]), m * jnp.sin(li * n[:, None])

    j = jnp.arange(S5_CHUNK, dtype=F32)
    tabs = jnp.stack([*powers(-j), *powers(j), *powers(j + 1.0)], axis=0)
    return bcat, ccat, tabs


def _s5_mixer(ua, bsz, s, bcat, ccat, tabs, d_skip, w_glu):
    L = S5_CHUNK
    nc = s // L
    return pl.pallas_call(
        _s5_kernel,
        grid=(bsz, nc),
        in_specs=[pl.BlockSpec((L, S5_CH), lambda b, j: (b * nc + j, 0)),
                  pl.BlockSpec((S5_CH, 2 * S5_W), lambda b, j: (0, 0)),
                  pl.BlockSpec((2 * S5_W, S5_CH), lambda b, j: (0, 0)),
                  pl.BlockSpec((6, L, S5_W), lambda b, j: (0, 0, 0)),
                  pl.BlockSpec((1, S5_CH), lambda b, j: (0, 0)),
                  pl.BlockSpec((S5_CH, S5_CH), lambda b, j: (0, 0))],
        out_specs=pl.BlockSpec((L, S5_CH), lambda b, j: (b * nc + j, 0)),
        out_shape=jax.ShapeDtypeStruct((bsz * s, S5_CH), F32),
        scratch_shapes=[pltpu.VMEM((2, S5_W), F32)],
        compiler_params=_params(("parallel", "arbitrary")),
        name="s5_mixer",
    )(ua, bcat, ccat, tabs, d_skip, w_glu)


def _moba_prep_kernel(qkv_ref, gq_ref, gk_ref, cos_ref, sin_ref, qT_ref, k_ref, vT_ref, gate_ref, km_ref):
    j = pl.program_id(1)

    @pl.when(j == 0)
    def _():
        km_ref[...] = jnp.zeros_like(km_ref)

    x = qkv_ref[...]
    c = cos_ref[...]
    s = sin_ref[...]
    half = ROPE_DIMS // 2

    def prep(xT, g):
        y = xT * _rms_scale(xT, 0) * g
        x1 = y[0:half]
        x2 = y[half:ROPE_DIMS]
        return jnp.concatenate([x1 * c - x2 * s, x2 * c + x1 * s, y[ROPE_DIMS:]], axis=0)

    qT_all = x[:, 0:MOBA_W].T
    kT_all = x[:, MOBA_W:2 * MOBA_W].T
    vT_all = x[:, 2 * MOBA_W:3 * MOBA_W].T
    gq = gq_ref[...]
    gk = gk_ref[...]
    for h in range(MOBA_HEADS):
        sl = slice(h * HEAD_DIM, (h + 1) * HEAD_DIM)
        q = prep(qT_all[sl], gq) * (HEAD_DIM ** -0.5)
        qT_ref[h] = q.astype(BF16)
        gate_ref[h] = _dot3(km_ref[h], q)
        k = prep(kT_all[sl], gk).T
        k_ref[h] = k.astype(BF16)
        km_ref[h, pl.ds(j, 1), :] = jnp.mean(k, axis=0, keepdims=True)
        vT_ref[h] = vT_all[sl].astype(BF16)


def _moba_prep(qkv, bsz, s, gq, gk, cosT, sinT):
    nb = s // MOBA_BLOCK
    H, dh, blk = MOBA_HEADS, HEAD_DIM, MOBA_BLOCK
    return pl.pallas_call(
        _moba_prep_kernel,
        grid=(bsz, nb),
        in_specs=[pl.BlockSpec((blk, 3 * MOBA_W), lambda b, j: (b * nb + j, 0)),
                  pl.BlockSpec((dh, 1), lambda b, j: (0, 0)),
                  pl.BlockSpec((dh, 1), lambda b, j: (0, 0)),
                  pl.BlockSpec((ROPE_DIMS // 2, blk), lambda b, j: (0, j)),
                  pl.BlockSpec((ROPE_DIMS // 2, blk), lambda b, j: (0, j))],
        out_specs=[pl.BlockSpec((None, H, dh, blk), lambda b, j: (b, 0, 0, j)),
                   pl.BlockSpec((None, H, None, blk, dh), lambda b, j: (b, 0, j, 0, 0)),
                   pl.BlockSpec((None, H, None, dh, blk), lambda b, j: (b, 0, j, 0, 0)),
                   pl.BlockSpec((None, H, nb, blk), lambda b, j: (b, 0, 0, j))],
        out_shape=[jax.ShapeDtypeStruct((bsz, H, dh, s), BF16),
                   jax.ShapeDtypeStruct((bsz, H, nb, blk, dh), BF16),
                   jax.ShapeDtypeStruct((bsz, H, nb, dh, blk), BF16),
                   jax.ShapeDtypeStruct((bsz, H, nb, s), F32)],
        scratch_shapes=[pltpu.VMEM((H, nb, dh), F32)],
        compiler_params=_params(("parallel", "arbitrary")),
        name="moba_prep",
    )(qkv, gq, gk, cosT, sinT)


def _moba_attn_kernel(qT_ref, k_ref, vT_ref, gate_ref, o_ref, msk_ref):
    i = pl.program_id(2)
    hp, nb = gate_ref.shape[0], gate_ref.shape[1]
    blk = MOBA_BLOCK
    row = lax.broadcasted_iota(jnp.int32, (nb, blk), 0)
    kpos = lax.broadcasted_iota(jnp.int32, (blk, blk), 0)
    qpos = lax.broadcasted_iota(jnp.int32, (blk, blk), 1)
    qTs, init = [], []
    for h in range(hp):
        qT = qT_ref[h]
        gm = jnp.where(row < i, gate_ref[h], NEG_INF)
        cnt = jnp.zeros((nb, blk), jnp.int32)
        for m in range(nb):
            gmm = gm[m:m + 1, :]
            beats = jnp.where(gmm > gm, 1, jnp.where(gmm == gm, jnp.where(row > m, 1, 0), 0))
            cnt = cnt + beats
        msk_ref[h] = jnp.where(row < i, jnp.where(cnt < MOBA_TOPK, 0.0, NEG_INF), NEG_INF)
        qTs.append(qT)
        init.append((jnp.full((1, blk), NEG_INF, F32), jnp.zeros((1, blk), F32),
                     jnp.zeros((HEAD_DIM, blk), F32)))

    U = MOBA_GROUP
    causal = jnp.where(kpos <= qpos, 0.0, NEG_INF)

    def body(g, carry):
        first = g == 0
        blocks = [jnp.where(first, i, g * U - 1)] + [g * U + u - 1 for u in range(1, U)]
        out = []
        for h in range(hp):
            m, l, acc = carry[h]
            biases = [jnp.where(first, causal, msk_ref[h, pl.ds(blocks[0], 1), :])]
            biases += [msk_ref[h, pl.ds(n, 1), :] for n in blocks[1:]]
            sTs = [jnp.dot(k_ref[h, n], qTs[h], preferred_element_type=F32) + b
                   for n, b in zip(blocks, biases)]
            m_new = m
            for sT in sTs:
                m_new = jnp.maximum(m_new, jnp.max(sT, axis=0, keepdims=True))
            alpha = jnp.exp(m - m_new)
            l = alpha * l
            acc = alpha * acc
            for n, sT in zip(blocks, sTs):
                p = jnp.exp(sT - m_new)
                l = l + jnp.sum(p, axis=0, keepdims=True)
                acc = acc + jnp.dot(vT_ref[h, n], p.astype(BF16), preferred_element_type=F32)
            out.append((m_new, l, acc))
        return tuple(out)

    groups = lax.shift_right_logical(i + U, U.bit_length() - 1)
    final = lax.fori_loop(0, groups, body, tuple(init))
    for h in range(hp):
        _, l, acc = final[h]
        o_ref[h] = acc / l


def _moba_attn(qT, k, vT, gate):
    bsz, H, dh, s = qT.shape
    nb = s // MOBA_BLOCK
    blk = MOBA_BLOCK
    hp = MOBA_HEADS_PER_STEP
    return pl.pallas_call(
        _moba_attn_kernel,
        grid=(bsz, H // hp, nb),
        in_specs=[pl.BlockSpec((None, hp, dh, blk), lambda b, h, i: (b, h, 0, i)),
                  pl.BlockSpec((None, hp, nb, blk, dh), lambda b, h, i: (b, h, 0, 0, 0)),
                  pl.BlockSpec((None, hp, nb, dh, blk), lambda b, h, i: (b, h, 0, 0, 0)),
                  pl.BlockSpec((None, hp, nb, blk), lambda b, h, i: (b, h, 0, i))],
        out_specs=pl.BlockSpec((None, hp, dh, blk), lambda b, h, i: (b, h, 0, i)),
        out_shape=jax.ShapeDtypeStruct((bsz, H, dh, s), F32),
        scratch_shapes=[pltpu.VMEM((hp, nb, blk), F32)],
        compiler_params=_params(("parallel", "parallel", "arbitrary")),
        name="moba_attn",
    )(qT, k, vT, gate)


def _ret_log_decay():
    return np.log(1.0 - 2.0 ** (-5.0 - np.arange(RET_HEADS, dtype=np.float64)))


def _ret_kernel(x_ref, cos_ref, sin_ref, xi_ref, zeta_ref, dm_ref, o_ref, st_ref):
    C = RET_CHUNK

    @pl.when(pl.program_id(1) == 0)
    def _():
        st_ref[...] = jnp.zeros_like(st_ref)

    x = x_ref[...]
    q = x[:, 0:RET_W]
    k = x[:, RET_W:2 * RET_W]
    v = x[:, 2 * RET_W:3 * RET_W]
    g = x[:, 3 * RET_W:4 * RET_W]
    cos = cos_ref[...]
    sin = sin_ref[...]
    lane = lax.broadcasted_iota(jnp.int32, (C, RET_W), 1)
    first = (lane & (HEAD_DIM // 2)) == 0

    def rot(t):
        swapped = jnp.where(first, pltpu.roll(t, RET_W - HEAD_DIM // 2, 1), pltpu.roll(t, HEAD_DIM // 2, 1))
        return t * cos + swapped * sin

    qr = rot(q)
    kr = rot(k) * (HEAD_DIM ** -0.5)
    qb = qr.astype(BF16)
    kb = kr.astype(BF16)
    vb = v.astype(BF16)
    qx = (qr * xi_ref[...]).astype(BF16)
    kz = kr * zeta_ref[...]
    decay_c = np.exp(C * _ret_log_decay())
    for h in range(RET_HEADS):
        sl = slice(h * HEAD_DIM, (h + 1) * HEAD_DIM)
        sc = lax.dot_general(qb[:, sl], kb[:, sl], (((1,), (1,)), ((), ())),
                             preferred_element_type=F32) * dm_ref[h]
        inner = jnp.dot(sc.astype(BF16), vb[:, sl], preferred_element_type=F32)
        r_prev = st_ref[h]
        cross = jnp.dot(qx[:, sl], r_prev.astype(BF16), preferred_element_type=F32)
        kv = jnp.dot(kz[:, sl].T.astype(BF16), vb[:, sl], preferred_element_type=F32)
        st_ref[h] = float(decay_c[h]) * r_prev + kv
        o = inner + cross
        o = o * _rms_scale(o, -1)
        gh = g[:, sl]
        o_ref[:, sl] = gh * jax.nn.sigmoid(gh) * o


def _ret_tables(s):
    C = RET_CHUNK
    half = HEAD_DIM // 2
    pos = jnp.arange(s, dtype=F32)
    inv = RET_ANGLE_BASE ** (-jnp.linspace(0.0, 1.0, half, dtype=F32))
    ang = pos[:, None] * inv[None, :]
    cos, sin = jnp.cos(ang), jnp.sin(ang)
    cos_t = jnp.tile(cos, (1, 2 * RET_HEADS))
    sin_t = jnp.tile(jnp.concatenate([-sin, sin], axis=1), (1, RET_HEADS))
    lg = _ret_log_decay()
    i = np.arange(C, dtype=np.float64)
    xi = np.repeat(np.exp((i + 1.0)[:, None] * lg[None, :]), HEAD_DIM, axis=1)
    zeta = np.repeat(np.exp((C - 1.0 - i)[:, None] * lg[None, :]), HEAD_DIM, axis=1)
    diff = i[:, None] - i[None, :]
    dm = np.where(diff >= 0, np.exp(np.maximum(diff, 0.0)[None] * lg[:, None, None]), 0.0)
    return cos_t, sin_t, jnp.asarray(xi, F32), jnp.asarray(zeta, F32), jnp.asarray(dm, F32)


def _ret_mixer(rt, bsz, s, tables):
    C = RET_CHUNK
    nc = s // C
    cos_t, sin_t, xi, zeta, dm = tables
    return pl.pallas_call(
        _ret_kernel,
        grid=(bsz, nc),
        in_specs=[pl.BlockSpec((C, 4 * RET_W), lambda b, j: (b * nc + j, 0)),
                  pl.BlockSpec((C, RET_W), lambda b, j: (j, 0)),
                  pl.BlockSpec((C, RET_W), lambda b, j: (j, 0)),
                  pl.BlockSpec((C, RET_W), lambda b, j: (0, 0)),
                  pl.BlockSpec((C, RET_W), lambda b, j: (0, 0)),
                  pl.BlockSpec((RET_HEADS, C, C), lambda b, j: (0, 0, 0))],
        out_specs=pl.BlockSpec((C, RET_W), lambda b, j: (b * nc + j, 0)),
        out_shape=jax.ShapeDtypeStruct((bsz * s, RET_W), F32),
        scratch_shapes=[pltpu.VMEM((RET_HEADS, HEAD_DIM, HEAD_DIM), F32)],
        compiler_params=_params(("parallel", "arbitrary")),
        name="ret_mixer",
    )(rt, cos_t, sin_t, xi, zeta, dm)


def _out_proj_kernel(x_ref, ya_ref, ot_ref, yc_ref, g_ref, w_ref, o_ref):
    ts = x_ref.shape[0]
    ot = ot_ref[...]
    yb = (ot * _rms_scale(ot, 1)).reshape(MOBA_W, ts).T
    g = g_ref[...]
    a0, a1 = S5_CH, S5_CH + MOBA_W
    acc = x_ref[...]
    acc = acc + jnp.dot((ya_ref[...] * g[:, 0:a0]).astype(BF16), w_ref[0:a0, :], preferred_element_type=F32)
    acc = acc + jnp.dot((yb * g[:, a0:a1]).astype(BF16), w_ref[a0:a1, :], preferred_element_type=F32)
    acc = acc + jnp.dot((yc_ref[...] * g[:, a1:MIX_W]).astype(BF16), w_ref[a1:MIX_W, :],
                        preferred_element_type=F32)
    o_ref[...] = acc


def _out_proj(x2, ya, ot, yc, g, w, bsz, s):
    ts = PROJ_ROWS
    nt = s // ts
    return pl.pallas_call(
        _out_proj_kernel,
        grid=(bsz, nt),
        in_specs=[pl.BlockSpec((ts, D_MODEL), lambda b, j: (b * nt + j, 0)),
                  pl.BlockSpec((ts, S5_CH), lambda b, j: (b * nt + j, 0)),
                  pl.BlockSpec((None, MOBA_HEADS, HEAD_DIM, ts), lambda b, j: (b, 0, 0, j)),
                  pl.BlockSpec((ts, RET_W), lambda b, j: (b * nt + j, 0)),
                  pl.BlockSpec((1, MIX_W), lambda b, j: (0, 0)),
                  pl.BlockSpec((MIX_W, D_MODEL), lambda b, j: (0, 0))],
        out_specs=pl.BlockSpec((ts, D_MODEL), lambda b, j: (b * nt + j, 0)),
        out_shape=jax.ShapeDtypeStruct((bsz * s, D_MODEL), F32),
        compiler_params=_params(("parallel", "parallel")),
        name="out_proj",
    )(x2, ya, ot, yc, g, w)


def _top16(s, rank, payload=None):
    vals, pays = [], []
    for _ in range(PEER_TOPK):
        m = jnp.max(s, axis=0, keepdims=True)
        pos = jnp.min(jnp.where(s == m, rank, jnp.int32(2 ** 30)), axis=0, keepdims=True)
        hit = rank == pos
        vals.append(m)
        pays.append(pos if payload is None else jnp.sum(jnp.where(hit, payload, 0), axis=0, keepdims=True))
        s = jnp.where(hit, NEG_INF, s)
    return jnp.concatenate(vals, axis=0), jnp.concatenate(pays, axis=0)


def _pair_candidates(v0, i0, v1, i1):
    K, n = PEER_TOPK, v0.shape[1]
    j = lax.broadcasted_iota(jnp.int32, (8, n), 0)
    vals, flat, eid = [], [], []
    for a in range(4):
        vals.append(v0[a:a + 1] + v1[0:8])
        flat.append(j + a * K)
        eid.append(i0[a:a + 1] * PEER_KEYS + i1[0:8])
    vals.append(v0[0:1] + v1[8:16])
    flat.append(j + 8)
    eid.append(i0[0:1] * PEER_KEYS + i1[8:16])
    vals.append(v0[8:16] + v1[0:1])
    flat.append((j + 8) * K)
    eid.append(i0[8:16] * PEER_KEYS + i1[0:1])
    for b in range(3):
        vals.append(jnp.where(j >= 4, v0[0:8] + v1[b:b + 1], NEG_INF))
        flat.append(jnp.where(j >= 4, j * K + b, K * K + j * K + b))
        eid.append(i0[0:8] * PEER_KEYS + i1[b:b + 1])
    return jnp.concatenate(vals, axis=0), jnp.concatenate(flat, axis=0), jnp.concatenate(eid, axis=0)


def _peer_route_kernel(x_ref, g_ref, wqT_ref, sk_ref, h2_ref, idx_ref, gate_ref, q_scr):
    tb = x_ref.shape[0]
    x = x_ref[...]
    h2 = x * _rms_scale(x, -1) * g_ref[...]
    h2_ref[...] = h2
    q_scr[...] = lax.dot_general(wqT_ref[...], h2.astype(BF16), (((1,), (1,)), ((), ())),
                                 preferred_element_type=F32).astype(BF16)
    half = PEER_QDIM // 2
    io_keys = lax.broadcasted_iota(jnp.int32, (PEER_KEYS, tb), 0)

    def head(h, carry):
        sub = []
        for c in range(2):
            grp = 2 * h + c
            qg = q_scr[pl.ds(pl.multiple_of(grp * half, half), half), :]
            sc = jnp.dot(sk_ref[grp], qg, preferred_element_type=F32)
            sub.append(_top16(sc, io_keys))
        (v0, i0), (v1, i1) = sub
        fs, eidx = _top16(*_pair_candidates(v0, i0, v1, i1))
        e = jnp.exp(fs - fs[0:1, :])
        gates = e / jnp.sum(e, axis=0, keepdims=True)
        r0 = pl.multiple_of(h * PEER_TOPK, PEER_TOPK)
        idx_ref[pl.ds(r0, PEER_TOPK), :] = eidx
        gate_ref[pl.ds(r0, PEER_TOPK), :] = gates
        return carry

    lax.fori_loop(0, PEER_HEADS, head, 0)


def _peer_route(x2, g, wqT, sk):
    t = x2.shape[0]
    tb = PEER_ROUTE_TOKENS
    ngrp = 2 * PEER_HEADS
    return pl.pallas_call(
        _peer_route_kernel,
        grid=(t // tb,),
        in_specs=[pl.BlockSpec((tb, D_MODEL), lambda i: (i, 0)),
                  pl.BlockSpec((1, D_MODEL), lambda i: (0, 0)),
                  pl.BlockSpec((PEER_HEADS * PEER_QDIM, D_MODEL), lambda i: (0, 0)),
                  pl.BlockSpec((ngrp, PEER_KEYS, PEER_QDIM // 2), lambda i: (0, 0, 0))],
        out_specs=[pl.BlockSpec((tb, D_MODEL), lambda i: (i, 0)),
                   pl.BlockSpec((PEER_SEL, tb), lambda i: (0, i)),
                   pl.BlockSpec((PEER_SEL, tb), lambda i: (0, i))],
        out_shape=[jax.ShapeDtypeStruct((t, D_MODEL), F32),
                   jax.ShapeDtypeStruct((PEER_SEL, t), jnp.int32),
                   jax.ShapeDtypeStruct((PEER_SEL, t), F32)],
        scratch_shapes=[pltpu.VMEM((PEER_HEADS * PEER_QDIM, tb), BF16)],
        compiler_params=_params(("parallel",)),
        name="peer_route",
    )(x2, g, wqT, sk)


def _pack_table(tab):
    bits = lax.bitcast_convert_type(tab.astype(BF16), jnp.uint16).astype(jnp.uint32)
    half = D_MODEL // 2
    words = bits[:, :half] | (bits[:, half:] << 16)
    return lax.bitcast_convert_type(words, jnp.int32).reshape(tab.shape[0] * TABLE_ROWS_PER_EXPERT, LANES)


def _table_spec(tab):
    return pl.BlockSpec(tab.shape, lambda i: (0, 0), pipeline_mode=pl.Buffered(1))


def _gather_rows(idx_ref, t, tab_ref, tile_ref):
    R = TABLE_ROWS_PER_EXPERT
    for k in range(PEER_SEL):
        row = pl.multiple_of(idx_ref[t, k], R)
        tile_ref[pl.ds(k, R, stride=TILE_STRIDE), :] = tab_ref[pl.ds(row, R), :]


def _tile_chunk(tile_ref, r):
    return pltpu.bitcast(tile_ref[r * TILE_STRIDE:r * TILE_STRIDE + PEER_SEL, :], BF16)


def _pipelined_tokens(tb, idx_ref, tab_ref, tiles, compute, finish, init):
    tile_a, tile_b = tiles
    _gather_rows(idx_ref, 0, tab_ref, tile_a)
    _gather_rows(idx_ref, 1, tab_ref, tile_b)

    def pair(i, pending):
        t0 = 2 * i
        finish(jnp.maximum(t0 - 1, 0), pending)
        va = compute(t0, tile_a)
        _gather_rows(idx_ref, jnp.minimum(t0 + 2, tb - 1), tab_ref, tile_a)
        finish(t0, va)
        vb = compute(t0 + 1, tile_b)
        _gather_rows(idx_ref, jnp.minimum(t0 + 3, tb - 1), tab_ref, tile_b)
        return vb

    last = lax.fori_loop(0, tb // 2, pair, init)
    finish(tb - 1, last)


def _hi_lo_rows(v, lane_parity):
    n = v.shape[1]
    vb = jnp.broadcast_to(v, (8, n))
    hi = vb.astype(BF16).astype(F32)
    srow = lax.broadcasted_iota(jnp.int32, (8, n), 0)
    part = jnp.where(srow < 2, hi, vb - hi)
    part = jnp.where(srow < 4, part, 0.0)
    return jnp.where((srow & 1) == lane_parity, part, 0.0)


def _peer_act_kernel(idx_ref, x_ref, gate_ref, tab_ref, c_ref, tile_a, tile_b):
    tb = x_ref.shape[0]
    R = TABLE_ROWS_PER_EXPERT
    half = D_MODEL // 2
    lane = lax.broadcasted_iota(jnp.int32, (PEER_SEL, tb), 1)
    c_ref[...] = jnp.zeros_like(c_ref)

    def compute(t, tile_ref):
        x = x_ref[pl.ds(t, 1), :]
        acc = jnp.zeros((PEER_SEL, LANES), F32)
        for r in range(R):
            w = tile_ref[r * TILE_STRIDE:r * TILE_STRIDE + PEER_SEL, :]
            lo = lax.bitcast_convert_type(w << 16, F32)
            hi = lax.bitcast_convert_type(w & jnp.int32(-65536), F32)
            acc = acc + lo * x[:, r * LANES:(r + 1) * LANES]
            acc = acc + hi * x[:, half + r * LANES:half + (r + 1) * LANES]
        return jnp.sum(acc, axis=1, keepdims=True)

    def finish(t, col):
        c_ref[...] = jnp.where(lane == t, col, c_ref[...])

    _pipelined_tokens(tb, idx_ref, tab_ref, (tile_a, tile_b), compute, finish,
                      jnp.zeros((PEER_SEL, 1), F32))
    c_ref[...] = gate_ref[...] * _gelu_tanh(c_ref[...])


def _tile_scratch():
    shape = (TABLE_ROWS_PER_EXPERT * TILE_STRIDE, LANES)
    return [pltpu.VMEM(shape, jnp.int32), pltpu.VMEM(shape, jnp.int32)]


def _peer_act(idx, h2, gate_t, tab):
    t = h2.shape[0]
    tb = PEER_GATHER_TOKENS
    return pl.pallas_call(
        _peer_act_kernel,
        grid=(t // tb,),
        in_specs=[pl.BlockSpec((tb, PEER_SEL), lambda i: (i, 0), memory_space=pltpu.SMEM),
                  pl.BlockSpec((tb, D_MODEL), lambda i: (i, 0)),
                  pl.BlockSpec((PEER_SEL, tb), lambda i: (0, i)),
                  _table_spec(tab)],
        out_specs=pl.BlockSpec((PEER_SEL, tb), lambda i: (0, i)),
        out_shape=jax.ShapeDtypeStruct((PEER_SEL, t), F32),
        scratch_shapes=_tile_scratch(),
        compiler_params=_params(("arbitrary",)),
        name="peer_act",
    )(idx, h2, gate_t, tab)


def _peer_out_kernel(idx_ref, c_ref, tab_ref, o_ref, tile_a, tile_b):
    tb = o_ref.shape[0]
    R = TABLE_ROWS_PER_EXPERT
    half = D_MODEL // 2
    lane_parity = lax.broadcasted_iota(jnp.int32, (8, 2 * PEER_SEL), 1) & 1

    def compute(t, tile_ref):
        rows = _hi_lo_rows(c_ref[t], lane_parity).astype(BF16)
        res = [jnp.dot(rows, _tile_chunk(tile_ref, r), preferred_element_type=F32) for r in range(R)]
        first = jnp.concatenate([v[0:1] + v[2:3] for v in res], axis=1)
        second = jnp.concatenate([v[1:2] + v[3:4] for v in res], axis=1)
        return jnp.concatenate([first, second], axis=1)

    def finish(t, row):
        o_ref[t] = row

    _pipelined_tokens(tb, idx_ref, tab_ref, (tile_a, tile_b), compute, finish,
                      jnp.zeros((1, D_MODEL), F32))


def _peer_out(idx, c_i, tab):
    t = idx.shape[0]
    tb = PEER_GATHER_TOKENS
    return pl.pallas_call(
        _peer_out_kernel,
        grid=(t // tb,),
        in_specs=[pl.BlockSpec((tb, PEER_SEL), lambda i: (i, 0), memory_space=pltpu.SMEM),
                  pl.BlockSpec((tb, 1, 2 * PEER_SEL), lambda i: (i, 0, 0)),
                  _table_spec(tab)],
        out_specs=pl.BlockSpec((tb, 1, D_MODEL), lambda i: (i, 0, 0)),
        out_shape=jax.ShapeDtypeStruct((t, 1, D_MODEL), F32),
        scratch_shapes=_tile_scratch(),
        compiler_params=_params(("arbitrary",)),
        name="peer_out",
    )(idx, c_i, tab)


def _mixer_layer(x2, bsz, s, p, rope, ret_tables):
    ua, mb, rt = _in_proj(x2, p['norm1_g'], p['w_in'], p['w_in_qk_lo'])
    ya = _s5_mixer(ua, bsz, s, p['s5_bcat'], p['s5_ccat'], p['s5_tabs'], p['s5_d'], p['s5_w_glu'])
    qT, k, vT, gate = _moba_prep(mb, bsz, s, p['moba_q_g'], p['moba_k_g'], *rope)
    ot = _moba_attn(qT, k, vT, gate)
    yc = _ret_mixer(rt, bsz, s, ret_tables)
    return _out_proj(x2, ya, ot, yc, p['mix_out_g'], p['w_out'], bsz, s)


def _peer_layer(x2, p):
    h2, idx_t, gate_t = _peer_route(x2, p['norm2_g'], p['peer_wqT'], p['peer_sk'])
    idx = idx_t.T * TABLE_ROWS_PER_EXPERT
    c_t = _peer_act(idx, h2, gate_t, p['peer_u'])
    c_i = jnp.repeat(c_t.T, 2, axis=1)[:, None, :]
    return x2 + _peer_out(idx, c_i, p['peer_v']).reshape(x2.shape)


def kernel(x, norm1_g, w_in, s5_lam_re, s5_lam_im, s5_log_dt, s5_b_re, s5_b_im, s5_c_re, s5_c_im,
           s5_d, s5_w_glu, moba_q_g, moba_k_g, mix_out_g, w_out, norm2_g, peer_w_q, peer_sub_keys,
           peer_u, peer_v):
    bsz, s, d = x.shape
    depth = w_in.shape[0]
    pos = jnp.arange(s, dtype=F32)
    inv = ROPE_THETA ** (-jnp.arange(0, ROPE_DIMS, 2, dtype=F32) / ROPE_DIMS)
    ang = (pos[:, None] * inv[None, :]).T
    rope = (jnp.cos(ang), jnp.sin(ang))
    ret_tables = _ret_tables(s)
    x2 = x.reshape(bsz * s, d)
    for l in range(depth):
        bcat, ccat, tabs = _s5_operands(s5_lam_re[l], s5_lam_im[l], s5_log_dt[l], s5_b_re[l],
                                        s5_b_im[l], s5_c_re[l], s5_c_im[l])
        w_in_hi = w_in[l].astype(BF16)
        qk_cols = slice(S5_CH, S5_CH + 2 * MOBA_W)
        p = {
            'norm1_g': norm1_g[l][None, :],
            'w_in': w_in_hi,
            'w_in_qk_lo': (w_in[l][:, qk_cols] - w_in_hi[:, qk_cols].astype(F32)).astype(BF16),
            's5_bcat': bcat, 's5_ccat': ccat, 's5_tabs': tabs,
            's5_d': s5_d[l][None, :],
            's5_w_glu': s5_w_glu[l].astype(BF16),
            'moba_q_g': moba_q_g[l][:, None],
            'moba_k_g': moba_k_g[l][:, None],
            'mix_out_g': mix_out_g[l][None, :],
            'w_out': w_out[l].astype(BF16),
            'norm2_g': norm2_g[l][None, :],
            'peer_wqT': peer_w_q[l].T.astype(BF16),
            'peer_sk': peer_sub_keys[l].reshape(2 * PEER_HEADS, PEER_KEYS, PEER_QDIM // 2).astype(BF16),
            'peer_u': _pack_table(peer_u[l]),
            'peer_v': _pack_table(peer_v[l]),
        }
        x2 = _mixer_layer(x2, bsz, s, p, rope, ret_tables)
        x2 = _peer_layer(x2, p)
    return x2.reshape(bsz, s, d)
```
